```python
import math
import jax
import jax.numpy as jnp
from jax import lax
import numpy as np

D_MODEL = 4096
BATCH = 2
SEQ = 4096
DEPTH = 2

N_BRANCH = 4
BRANCH_WIDTH = D_MODEL // 4

GDN_HEAD_DIM = 128
GDN_HEADS = BRANCH_WIDTH // GDN_HEAD_DIM
GDN_WIDTH = GDN_HEADS * GDN_HEAD_DIM
GDN_CHUNK = 64
SHORT_CONV = 4

SWA_HEAD_DIM = 128
SWA_Q_HEADS = BRANCH_WIDTH // SWA_HEAD_DIM
SWA_KV_HEADS = SWA_Q_HEADS // 4
SWA_WIDTH = SWA_Q_HEADS * SWA_HEAD_DIM
SWA_WINDOW = 128
SWA_BLOCK = SWA_WINDOW
ROPE_THETA = 10000.0

CONF_WIDTH = BRANCH_WIDTH
CONF_CONV_WIDTH = 31

SSD_HEAD_DIM = 64
SSD_WIDTH = BRANCH_WIDTH
SSD_HEADS = SSD_WIDTH // SSD_HEAD_DIM
SSD_GROUPS = 2
SSD_STATE = 128
SSD_CHUNK = 64
SSD_XBC_WIDTH = SSD_WIDTH + 2 * SSD_GROUPS * SSD_STATE

FFN_HIDDEN = -(-8 * D_MODEL // (3 * 256)) * 256

ALPHA = (2.0 * DEPTH) ** 0.25
BETA = (8.0 * DEPTH) ** -0.25
LN_EPS = 1e-5
RMS_EPS = 1e-6

IN_SPLIT_SIZES = (
    3 * GDN_WIDTH,
    GDN_WIDTH,
    GDN_HEADS,
    GDN_HEADS,
    SWA_WIDTH,
    SWA_KV_HEADS * SWA_HEAD_DIM,
    SWA_KV_HEADS * SWA_HEAD_DIM,
    2 * CONF_WIDTH,
    SSD_WIDTH,
    SSD_XBC_WIDTH,
    SSD_HEADS,
    N_BRANCH * D_MODEL,
)
IN_COLS = sum(IN_SPLIT_SIZES)

kernel_name = 'hybrid_gdn_swa_conformer_ssd_deepnorm'


def layer_norm(x, g, b):
    xf = x.astype(jnp.float32)
    mu = jnp.mean(xf, axis=-1, keepdims=True)
    var = jnp.mean(jnp.square(xf - mu), axis=-1, keepdims=True)
    return ((xf - mu) * lax.rsqrt(var + LN_EPS) * g + b).astype(x.dtype)


def rms_norm(x, g):
    xf = x.astype(jnp.float32)
    return (xf * lax.rsqrt(jnp.mean(xf * xf, axis=-1, keepdims=True) + RMS_EPS) * g).astype(x.dtype)


def l2_normalize(x):
    xf = x.astype(jnp.float32)
    return xf * lax.rsqrt(jnp.sum(xf * xf, axis=-1, keepdims=True) + RMS_EPS)


def causal_depthwise_conv(x, w):
    k = w.shape[0]
    return lax.conv_general_dilated(
        x, w.astype(x.dtype)[:, None, :], window_strides=(1,), padding=[(k - 1, 0)],
        dimension_numbers=('NWC', 'WIO', 'NWC'), feature_group_count=x.shape[-1])


def rotary_tables(t):
    inv_freq = ROPE_THETA ** (-jnp.arange(0, SWA_HEAD_DIM, 2, dtype=jnp.float32) / SWA_HEAD_DIM)
    ang = jnp.arange(t, dtype=jnp.float32)[:, None] * inv_freq[None, :]
    return jnp.cos(ang), jnp.sin(ang)


def rotary(x, cos, sin):
    x1, x2 = jnp.split(x, 2, axis=-1)
    c, s = cos[:, None, :], sin[:, None, :]
    return jnp.concatenate([x1 * c - x2 * s, x2 * c + x1 * s], axis=-1).astype(x.dtype)


def chunked_gated_delta_rule(q, k, v, g, beta):
    bsz, t, h, d = q.shape
    n, c = t // GDN_CHUNK, GDN_CHUNK

    def to_chunks(z):
        return jnp.moveaxis(z.reshape((bsz, n, c, h) + z.shape[3:]), 3, 1)

    q, k, v, g, beta = (to_chunks(z) for z in (q, k, v, g, beta))
    gcum = jnp.cumsum(g, axis=-1)
    causal = jnp.tril(jnp.ones((c, c), dtype=bool))
    strict = jnp.tril(jnp.ones((c, c), dtype=bool), -1)
    diff = gcum[..., :, None] - gcum[..., None, :]
    decay = jnp.where(causal, jnp.exp(jnp.where(causal, diff, 0.0)), 0.0)
    kb = k * beta[..., None]
    a_mat = jnp.where(strict, jnp.einsum('bhncd,bhnsd->bhncs', kb, k) * decay, 0.0)
    eye = jnp.eye(c, dtype=jnp.float32)
    t_mat = lax.linalg.triangular_solve(eye + a_mat, jnp.broadcast_to(eye, a_mat.shape),
                                        left_side=True, lower=True)
    u = jnp.einsum('bhncs,bhnsd->bhncd', t_mat, v * beta[..., None])
    w = jnp.einsum('bhncs,bhnsd->bhncd', t_mat, kb * jnp.exp(gcum)[..., None])
    qk = jnp.einsum('bhncd,bhnsd->bhncs', q, k) * decay
    q_dec = q * jnp.exp(gcum)[..., None]
    k_dec = k * jnp.exp(gcum[..., -1:] - gcum)[..., None]
    chunk_decay = jnp.exp(gcum[..., -1])

    def step(state, xs):
        u_c, w_c, qk_c, qd_c, kd_c, cd_c = xs
        v_new = u_c - jnp.einsum('bhcd,bhde->bhce', w_c, state)
        o_c = jnp.einsum('bhcd,bhde->bhce', qd_c, state) + jnp.einsum('bhcs,bhse->bhce', qk_c, v_new)
        state = state * cd_c[..., None, None] + jnp.einsum('bhcd,bhce->bhde', kd_c, v_new)
        return state, o_c

    xs = tuple(jnp.moveaxis(z, 2, 0) for z in (u, w, qk, q_dec, k_dec, chunk_decay))
    state0 = jnp.zeros((bsz, h, d, d), jnp.float32)
    _, o = lax.scan(step, state0, xs)
    return jnp.transpose(o, (1, 0, 3, 2, 4)).reshape(bsz, t, h, d)


def gated_deltanet(qkv, gate, a, b, conv_w, a_log, dt_bias, norm_w):
    bsz, t, _ = qkv.shape
    f32 = jnp.float32
    qkv = jax.nn.silu(causal_depthwise_conv(qkv, conv_w))
    q, k, v = jnp.split(qkv, 3, axis=-1)
    shp = (bsz, t, GDN_HEADS, GDN_HEAD_DIM)
    q = l2_normalize(q.reshape(shp)) * (GDN_HEAD_DIM ** -0.5)
    k = l2_normalize(k.reshape(shp))
    v = v.reshape(shp).astype(f32)
    beta = jax.nn.sigmoid(b.astype(f32))
    g = -jnp.exp(a_log.astype(f32)) * jax.nn.softplus(a.astype(f32) + dt_bias.astype(f32))
    o = chunked_gated_delta_rule(q, k, v, g, beta)
    o = rms_norm(o, norm_w) * jax.nn.silu(gate.reshape(shp).astype(f32))
    return o.reshape(bsz, t, GDN_WIDTH).astype(gate.dtype)


def sliding_window_attention(q, k, v, sinks, cos, sin):
    bsz, t, _ = q.shape
    nb = t // SWA_BLOCK
    rep = SWA_Q_HEADS // SWA_KV_HEADS
    q = rotary(q.reshape(bsz, t, SWA_Q_HEADS, SWA_HEAD_DIM), cos, sin)
    k = rotary(k.reshape(bsz, t, SWA_KV_HEADS, SWA_HEAD_DIM), cos, sin)
    v = v.reshape(bsz, t, SWA_KV_HEADS, SWA_HEAD_DIM)
    qb = q.reshape(bsz, nb, SWA_BLOCK, SWA_KV_HEADS, rep, SWA_HEAD_DIM)

    def band(z):
        zb = z.reshape(bsz, nb, SWA_BLOCK, SWA_KV_HEADS, SWA_HEAD_DIM)
        prev = jnp.pad(zb, ((0, 0), (1, 0), (0, 0), (0, 0), (0, 0)))[:, :-1]
        return jnp.concatenate([prev, zb], axis=2)

    kb, vb = band(k), band(v)
    s = jnp.einsum('bnqgrd,bnkgd->bngrqk', qb, kb).astype(jnp.float32) * (SWA_HEAD_DIM ** -0.5)
    qi = jnp.arange(SWA_BLOCK)[:, None]
    kj = jnp.arange(2 * SWA_BLOCK)[None, :]
    dist = qi + SWA_BLOCK - kj
    in_window = (dist >= 0) & (dist < SWA_WINDOW)
    has_prev = (jnp.arange(nb)[:, None, None] > 0) | (kj >= SWA_BLOCK)[None]
    mask = in_window[None] & has_prev
    s = jnp.where(mask[None, :, None, None], s, -jnp.inf)
    sink = sinks.astype(jnp.float32).reshape(1, 1, SWA_KV_HEADS, rep, 1, 1)
    m = jnp.maximum(jnp.max(s, axis=-1, keepdims=True), sink)
    p = jnp.exp(s - m)
    denom = jnp.sum(p, axis=-1, keepdims=True) + jnp.exp(sink - m)
    o = jnp.einsum('bngrqk,bnkgd->bnqgrd', (p / denom).astype(v.dtype), vb)
    return o.reshape(bsz, t, SWA_WIDTH)


def conformer_conv_module(u, conv_w, conv_b, ln_g, ln_b, pw_w, pw_b):
    a, gt = jnp.split(u, 2, axis=-1)
    h = a * jax.nn.sigmoid(gt)
    h = causal_depthwise_conv(h, conv_w) + conv_b
    h = jax.nn.silu(layer_norm(h, ln_g, ln_b))
    return h @ pw_w + pw_b


def ssd_chunked(x, a, bm, cm):
    bsz, t, g, e, p = x.shape
    n, c = t // SSD_CHUNK, SSD_CHUNK
    x = x.reshape(bsz, n, c, g, e, p)
    bm = bm.reshape(bsz, n, c, g, SSD_STATE)
    cm = cm.reshape(bsz, n, c, g, SSD_STATE)
    a = jnp.moveaxis(a.reshape(bsz, n, c, g, e), 2, -1)
    a_cs = jnp.cumsum(a, axis=-1)
    causal = jnp.tril(jnp.ones((c, c), dtype=bool))
    seg = jnp.exp(jnp.where(causal, a_cs[..., :, None] - a_cs[..., None, :], -jnp.inf))
    cb = jnp.einsum('bnlgk,bnsgk->bngls', cm, bm)
    y_diag = jnp.einsum('bngls,bngels,bnsgep->bnlgep', cb, seg, x)
    decay_states = jnp.exp(a_cs[..., -1:] - a_cs)
    states = jnp.einsum('bnsgk,bnges,bnsgep->bngepk', bm, decay_states, x)
    chunk_decay = jnp.exp(a_cs[..., -1])

    def step(h, inp):
        st, dec = inp
        return h * dec[..., None, None] + st, h

    h0 = jnp.zeros((bsz, g, e, p, SSD_STATE), jnp.float32)
    _, h_in = lax.scan(step, h0, (jnp.moveaxis(states, 1, 0), jnp.moveaxis(chunk_decay, 1, 0)))
    h_in = jnp.moveaxis(h_in, 0, 1)
    y_off = jnp.einsum('bnlgk,bngepk,bngel->bnlgep', cm, h_in, jnp.exp(a_cs))
    return (y_diag + y_off).reshape(bsz, t, g, e, p)


def mamba2_ssd(z, xbc, dt_raw, conv_w, conv_b, a_log, dt_bias, d_skip, norm_w):
    bsz, t, _ = z.shape
    f32 = jnp.float32
    e = SSD_HEADS // SSD_GROUPS
    xbc = jax.nn.silu(causal_depthwise_conv(xbc, conv_w) + conv_b)
    xs, bm, cm = jnp.split(xbc, [SSD_WIDTH, SSD_WIDTH + SSD_GROUPS * SSD_STATE], axis=-1)
    xs = xs.reshape(bsz, t, SSD_GROUPS, e, SSD_HEAD_DIM).astype(f32)
    bm = bm.reshape(bsz, t, SSD_GROUPS, SSD_STATE).astype(f32)
    cm = cm.reshape(bsz, t, SSD_GROUPS, SSD_STATE).astype(f32)
    dt = jax.nn.softplus(dt_raw.astype(f32) + dt_bias.astype(f32)).reshape(bsz, t, SSD_GROUPS, e)
    a = -jnp.exp(a_log.astype(f32)).reshape(SSD_GROUPS, e) * dt
    y = ssd_chunked(xs * dt[..., None], a, bm, cm)
    y = y + d_skip.astype(f32).reshape(SSD_GROUPS, e)[..., None] * xs
    y = y.reshape(bsz, t, SSD_WIDTH)
    y = rms_norm(y * jax.nn.silu(z.astype(f32)), norm_w)
    return y.astype(z.dtype)


def hybrid_mixer(x, w_in, gdn_conv_w, gdn_a_log, gdn_dt_bias, gdn_norm_w, swa_sinks,
                 conf_conv_w, conf_conv_b, conf_ln_g, conf_ln_b, conf_pw_w, conf_pw_b,
                 ssd_conv_w, ssd_conv_b, ssd_a_log, ssd_dt_bias, ssd_d, ssd_norm_w,
                 w_branch, w_out, cos, sin):
    bsz, t, _ = x.shape
    proj = x @ w_in
    split_at = np.cumsum(IN_SPLIT_SIZES)[:-1].tolist()
    (gdn_qkv, gdn_gate, gdn_a, gdn_b, swa_q, swa_k, swa_v, conf_u,
     ssd_z, ssd_xbc, ssd_dt, gate_logits) = jnp.split(proj, split_at, axis=-1)
    o_a = gated_deltanet(gdn_qkv, gdn_gate, gdn_a, gdn_b, gdn_conv_w, gdn_a_log, gdn_dt_bias, gdn_norm_w)
    o_b = sliding_window_attention(swa_q, swa_k, swa_v, swa_sinks, cos, sin)
    o_c = conformer_conv_module(conf_u, conf_conv_w, conf_conv_b, conf_ln_g, conf_ln_b, conf_pw_w, conf_pw_b)
    o_d = mamba2_ssd(ssd_z, ssd_xbc, ssd_dt, ssd_conv_w, ssd_conv_b, ssd_a_log, ssd_dt_bias, ssd_d, ssd_norm_w)
    branches = jnp.stack([o.astype(x.dtype) for o in (o_a, o_b, o_c, o_d)], axis=2)
    up = jnp.einsum('btiw,iwd->btid', branches, w_branch)
    gates = jax.nn.sigmoid(gate_logits.reshape(bsz, t, N_BRANCH, D_MODEL))
    merged = jnp.einsum('btid,btid->btd', gates, up)
    return merged @ w_out


def swiglu(x, w_gate_up, w_down):
    gt, up = jnp.split(x @ w_gate_up, 2, axis=-1)
    return (jax.nn.silu(gt) * up) @ w_down


def setup_inputs(seed: int = 0) -> dict:
    key = jax.random.key(seed)
    ks = jax.random.split(key, 27)
    f32 = jnp.float32
    L, D = DEPTH, D_MODEL

    def normal(k, shape, scale):
        return jax.random.normal(k, shape, f32) * scale

    def gain(k, shape):
        return 1.0 + 0.02 * jax.random.normal(k, shape, f32)

    def a_log_init(k, n):
        return jnp.log(jax.random.uniform(k, (L, n), f32, 1.0, 16.0))

    def dt_bias_init(k, n):
        dt = jnp.exp(jax.random.uniform(k, (L, n), f32, math.log(1e-3), math.log(1e-1)))
        return dt + jnp.log(-jnp.expm1(-dt))

    return {
        'x': normal(ks[0], (BATCH, SEQ, D), 1.0),
        'w_in': normal(ks[1], (L, D, IN_COLS), D ** -0.5),
        'gdn_conv_w': normal(ks[2], (L, SHORT_CONV, 3 * GDN_WIDTH), SHORT_CONV ** -0.5),
        'gdn_a_log': a_log_init(ks[3], GDN_HEADS),
        'gdn_dt_bias': dt_bias_init(ks[4], GDN_HEADS),
        'gdn_norm_w': gain(ks[5], (L, GDN_HEAD_DIM)),
        'swa_sinks': normal(ks[6], (L, SWA_Q_HEADS), 0.5),
        'conf_conv_w': normal(ks[7], (L, CONF_CONV_WIDTH, CONF_WIDTH), CONF_CONV_WIDTH ** -0.5),
        'conf_conv_b': normal(ks[8], (L, CONF_WIDTH), 0.02),
        'conf_ln_g': gain(ks[9], (L, CONF_WIDTH)),
        'conf_ln_b': normal(ks[10], (L, CONF_WIDTH), 0.02),
        'conf_pw_w': normal(ks[11], (L, CONF_WIDTH, CONF_WIDTH), CONF_WIDTH ** -0.5),
        'conf_pw_b': normal(ks[12], (L, CONF_WIDTH), 0.02),
        'ssd_conv_w': normal(ks[13], (L, SHORT_CONV, SSD_XBC_WIDTH), SHORT_CONV ** -0.5),
        'ssd_conv_b': normal(ks[14], (L, SSD_XBC_WIDTH), 0.02),
        'ssd_a_log': a_log_init(ks[15], SSD_HEADS),
        'ssd_dt_bias': dt_bias_init(ks[16], SSD_HEADS),
        'ssd_d': gain(ks[17], (L, SSD_HEADS)),
        'ssd_norm_w': gain(ks[18], (L, SSD_WIDTH)),
        'w_branch': normal(ks[19], (L, N_BRANCH, BRANCH_WIDTH, D), BETA * BRANCH_WIDTH ** -0.5),
        'w_out': normal(ks[20], (L, D, D), BETA * D ** -0.5),
        'ln1_g': gain(ks[21], (L, D)),
        'ln1_b': normal(ks[22], (L, D), 0.02),
        'w_gate_up': normal(ks[23], (L, D, 2 * FFN_HIDDEN), D ** -0.5),
        'w_down': normal(ks[24], (L, FFN_HIDDEN, D), BETA * FFN_HIDDEN ** -0.5),
        'ln2_g': gain(ks[25], (L, D)),
        'ln2_b': normal(ks[26], (L, D), 0.02),
    }


def reference(x, w_in, gdn_conv_w, gdn_a_log, gdn_dt_bias, gdn_norm_w, swa_sinks,
              conf_conv_w, conf_conv_b, conf_ln_g, conf_ln_b, conf_pw_w, conf_pw_b,
              ssd_conv_w, ssd_conv_b, ssd_a_log, ssd_dt_bias, ssd_d, ssd_norm_w,
              w_branch, w_out, ln1_g, ln1_b, w_gate_up, w_down, ln2_g, ln2_b):
    cos, sin = rotary_tables(x.shape[1])
    for i in range(DEPTH):
        h = hybrid_mixer(x, w_in[i], gdn_conv_w[i], gdn_a_log[i], gdn_dt_bias[i], gdn_norm_w[i],
                         swa_sinks[i], conf_conv_w[i], conf_conv_b[i], conf_ln_g[i], conf_ln_b[i],
                         conf_pw_w[i], conf_pw_b[i], ssd_conv_w[i], ssd_conv_b[i], ssd_a_log[i],
                         ssd_dt_bias[i], ssd_d[i], ssd_norm_w[i], w_branch[i], w_out[i], cos, sin)
        x = layer_norm(ALPHA * x + h, ln1_g[i], ln1_b[i])
        x = layer_norm(ALPHA * x + swiglu(x, w_gate_up[i], w_down[i]), ln2_g[i], ln2_b[i])
    return x
```

```python
import functools

import jax
import jax.numpy as jnp
import numpy as np
from jax import lax
from jax.experimental import pallas as pl
from jax.experimental.pallas import tpu as pltpu

F32 = jnp.float32
BF16 = jnp.bfloat16

D_MODEL = 4096
DEPTH = 2
N_BRANCH = 4
BRANCH_WIDTH = D_MODEL // 4

GDN_HEAD_DIM = 128
GDN_HEADS = BRANCH_WIDTH // GDN_HEAD_DIM
GDN_WIDTH = GDN_HEADS * GDN_HEAD_DIM
CHUNK = 64
SHORT_CONV = 4

SWA_HEAD_DIM = 128
SWA_Q_HEADS = BRANCH_WIDTH // SWA_HEAD_DIM
SWA_KV_HEADS = SWA_Q_HEADS // 4
SWA_REP = SWA_Q_HEADS // SWA_KV_HEADS
SWA_WINDOW = 128
ROPE_THETA = 10000.0

CONF_WIDTH = BRANCH_WIDTH
CONF_CONV_WIDTH = 31

SSD_HEAD_DIM = 64
SSD_WIDTH = BRANCH_WIDTH
SSD_HEADS = SSD_WIDTH // SSD_HEAD_DIM
SSD_GROUPS = 2
SSD_HEADS_PER_GROUP = SSD_HEADS // SSD_GROUPS
SSD_STATE = 128
SSD_BC_WIDTH = 2 * SSD_GROUPS * SSD_STATE

FFN_HIDDEN = -(-8 * D_MODEL // (3 * 256)) * 256

ALPHA = (2.0 * DEPTH) ** 0.25
LN_EPS = 1e-5
RMS_EPS = 1e-6

LANES = 128
VMEM_LIMIT = 52 * 1024 * 1024

COL_QKV = 0
COL_GATE = COL_QKV + 3 * GDN_WIDTH
COL_SWA_Q = COL_GATE + GDN_WIDTH
COL_CONF = COL_SWA_Q + BRANCH_WIDTH
COL_Z = COL_CONF + 2 * CONF_WIDTH
COL_XS = COL_Z + SSD_WIDTH
COL_BC = COL_XS + SSD_WIDTH
COL_SWA_K = COL_BC + SSD_BC_WIDTH
COL_SWA_V = COL_SWA_K + SWA_KV_HEADS * SWA_HEAD_DIM
COL_SMALL = COL_SWA_V + SWA_KV_HEADS * SWA_HEAD_DIM
SMALL_A = 0
SMALL_B = GDN_HEADS
SMALL_DT = 2 * GDN_HEADS
PROJ_COLS = 10752

_REF_SPLITS = (3 * GDN_WIDTH, GDN_WIDTH, GDN_HEADS, GDN_HEADS, BRANCH_WIDTH, 256, 256,
               2 * CONF_WIDTH, SSD_WIDTH, SSD_WIDTH + SSD_BC_WIDTH, SSD_HEADS, N_BRANCH * D_MODEL)
_REF_OFFS = np.concatenate([[0], np.cumsum(_REF_SPLITS)]).tolist()


def _params(sem, vmem=VMEM_LIMIT):
    return pltpu.CompilerParams(dimension_semantics=sem, vmem_limit_bytes=vmem)


def _bdot(a, b):
    return jnp.dot(a.astype(BF16), b.astype(BF16), preferred_element_type=F32)


def _bdot_nt(a, b):
    return lax.dot_general(a.astype(BF16), b.astype(BF16), (((1,), (1,)), ((), ())),
                           preferred_element_type=F32)


def _bdot_tn(a, b):
    return lax.dot_general(a.astype(BF16), b.astype(BF16), (((0,), (0,)), ((), ())),
                           preferred_element_type=F32)


def _silu(x):
    return x * jax.nn.sigmoid(x)


def _softplus(x):
    return jnp.maximum(x, 0.0) + jnp.log(1.0 + jnp.exp(-jnp.abs(x)))


def _mm_kernel(x_ref, w_ref, o_ref):
    o_ref[...] = jnp.dot(x_ref[...], w_ref[...], preferred_element_type=F32).astype(o_ref.dtype)


def _mm_resid_kernel(x_ref, w_ref, r_ref, o_ref):
    acc = jnp.dot(x_ref[...], w_ref[...], preferred_element_type=F32)
    o_ref[...] = ALPHA * r_ref[...] + acc


def _matmul(x, w, *, tm, tn, out_dtype, resid=None, name):
    m, k = x.shape
    n = w.shape[1]
    in_specs = [pl.BlockSpec((tm, k), lambda j, i: (i, 0)),
                pl.BlockSpec((k, tn), lambda j, i: (0, j))]
    args = [x, w]
    kern = _mm_kernel
    if resid is not None:
        in_specs.append(pl.BlockSpec((tm, tn), lambda j, i: (i, j)))
        args.append(resid)
        kern = _mm_resid_kernel
    return pl.pallas_call(
        kern, out_shape=jax.ShapeDtypeStruct((m, n), out_dtype),
        grid=(n // tn, m // tm), in_specs=in_specs,
        out_specs=pl.BlockSpec((tm, tn), lambda j, i: (i, j)),
        compiler_params=_params(("parallel", "parallel")), name=name)(*args)


def _swiglu_kernel(x_ref, wg_ref, wu_ref, o_ref):
    x = x_ref[...]
    g = jnp.dot(x, wg_ref[...], preferred_element_type=F32)
    u = jnp.dot(x, wu_ref[...], preferred_element_type=F32)
    o_ref[...] = (_silu(g) * u).astype(o_ref.dtype)


def _swiglu(x, w_gate_up, *, tm, tn):
    m, k = x.shape
    hidden = w_gate_up.shape[1] // 2
    nb = hidden // tn
    return pl.pallas_call(
        _swiglu_kernel, out_shape=jax.ShapeDtypeStruct((m, hidden), BF16),
        grid=(nb, m // tm),
        in_specs=[pl.BlockSpec((tm, k), lambda j, i: (i, 0)),
                  pl.BlockSpec((k, tn), lambda j, i: (0, j)),
                  pl.BlockSpec((k, tn), lambda j, i: (0, j + nb))],
        out_specs=pl.BlockSpec((tm, tn), lambda j, i: (i, j)),
        compiler_params=_params(("parallel", "parallel")), name="swiglu")(x, w_gate_up, w_gate_up)


def _merge_kernel(x_ref, oa_ref, ob_ref, oc_ref, od_ref,
                  wg0, wg1, wg2, wg3, wb0, wb1, wb2, wb3, o_ref):
    x = x_ref[...]
    acc = None
    for o_r, wg, wb in ((oa_ref, wg0, wb0), (ob_ref, wg1, wb1), (oc_ref, wg2, wb2), (od_ref, wg3, wb3)):
        gate = jax.nn.sigmoid(jnp.dot(x, wg[...], preferred_element_type=F32))
        up = jnp.dot(o_r[...], wb[...], preferred_element_type=F32)
        acc = gate * up if acc is None else acc + gate * up
    o_ref[...] = acc.astype(o_ref.dtype)


def _merge(x, branches, w_gates, w_branch, *, tm, tn):
    m, k = x.shape
    d = w_branch.shape[2]
    nb = d // tn
    bw = w_branch.shape[1]
    in_specs = [pl.BlockSpec((tm, k), lambda j, i: (i, 0))]
    in_specs += [pl.BlockSpec((tm, bw), lambda j, i: (i, 0)) for _ in range(N_BRANCH)]
    in_specs += [pl.BlockSpec((k, tn), functools.partial(lambda j, i, g: (0, g * nb + j), g=g))
                 for g in range(N_BRANCH)]
    in_specs += [pl.BlockSpec((None, bw, tn), functools.partial(lambda j, i, g: (g, 0, j), g=g))
                 for g in range(N_BRANCH)]
    return pl.pallas_call(
        _merge_kernel, out_shape=jax.ShapeDtypeStruct((m, d), BF16),
        grid=(nb, m // tm), in_specs=in_specs,
        out_specs=pl.BlockSpec((tm, tn), lambda j, i: (i, j)),
        compiler_params=_params(("parallel", "parallel")), name="merge",
    )(x, *branches, *([w_gates] * N_BRANCH), *([w_branch] * N_BRANCH))


def _ln_kernel(s_ref, g_ref, b_ref, of_ref, ob_ref):
    s = s_ref[...]
    mu = jnp.mean(s, axis=-1, keepdims=True)
    c = s - mu
    var = jnp.mean(c * c, axis=-1, keepdims=True)
    y = c * lax.rsqrt(var + LN_EPS) * g_ref[...] + b_ref[...]
    of_ref[...] = y
    ob_ref[...] = y.astype(BF16)


def _layer_norm(s, g, b, *, tr):
    m, d = s.shape
    row = pl.BlockSpec((tr, d), lambda i: (i, 0))
    vec = pl.BlockSpec((1, d), lambda i: (0, 0))
    return pl.pallas_call(
        _ln_kernel,
        out_shape=(jax.ShapeDtypeStruct((m, d), F32), jax.ShapeDtypeStruct((m, d), BF16)),
        grid=(m // tr,), in_specs=[row, vec, vec], out_specs=(row, row),
        compiler_params=_params(("parallel",)), name="layer_norm")(s, g.reshape(1, d), b.reshape(1, d))


def _causal_conv(pad_ref, x, w_ref, halo, t_first):
    rows = x.shape[0]
    k = w_ref.shape[0]

    @pl.when(t_first)
    def _():
        pad_ref[0:halo, :] = jnp.zeros((halo, x.shape[1]), F32)

    pad_ref[halo:halo + rows, :] = x
    acc = None
    for j in range(k):
        start = halo - (k - 1) + j
        term = w_ref[j:j + 1, :] * pad_ref[start:start + rows, :]
        acc = term if acc is None else acc + term
    pad_ref[0:halo, :] = x[rows - halo:rows, :]
    return acc


def _chunk_cumsum(x):
    r = lax.broadcasted_iota(jnp.int32, (CHUNK, CHUNK), 0)
    c = lax.broadcasted_iota(jnp.int32, (CHUNK, CHUNK), 1)
    tril = jnp.where(r >= c, 1.0, 0.0).astype(F32)
    return jnp.dot(tril, x, preferred_element_type=F32, precision=lax.Precision.HIGHEST)


def _unit_lower_inverse(a):
    r = lax.broadcasted_iota(jnp.int32, (CHUNK, CHUNK), 0)
    c = lax.broadcasted_iota(jnp.int32, (CHUNK, CHUNK), 1)
    same16 = (r >> 4) == (c >> 4)
    same32 = (r >> 5) == (c >> 5)
    eye = jnp.where(r == c, 1.0, 0.0).astype(F32)
    ad = jnp.where(same16, a, 0.0)
    o1 = jnp.where(same32, a, 0.0) - ad
    o2 = jnp.where(same32, 0.0, a)
    a2 = _bdot(ad, ad)
    a4 = _bdot(a2, a2)
    a8 = _bdot(a4, a4)
    d = eye - ad
    d = d + _bdot(d, a2)
    d = d + _bdot(d, a4)
    d = d + _bdot(d, a8)
    d = d - _bdot(d, _bdot(o1, d))
    d = d - _bdot(d, _bdot(o2, d))
    return d


def _causal_masks():
    r = lax.broadcasted_iota(jnp.int32, (CHUNK, CHUNK), 0)
    c = lax.broadcasted_iota(jnp.int32, (CHUNK, CHUNK), 1)
    return r >= c, r > c


def _decay_matrix(col, row, causal):
    return jnp.where(causal, jnp.exp(jnp.where(causal, col - row, 0.0)), 0.0)


def _gdn_kernel(qkv_ref, gate_ref, small_ref, convw_ref, nexpa_ref, dtb_ref, normw_ref,
                o_ref, pad_ref, state_ref, *, rows):
    t_first = pl.program_id(1) == 0

    @pl.when(t_first)
    def _():
        state_ref[...] = jnp.zeros(state_ref.shape, F32)

    y = _silu(_causal_conv(pad_ref, qkv_ref[...], convw_ref, 8, t_first))
    small = small_ref[...]
    g_all = nexpa_ref[...] * _softplus(small + dtb_ref[...])
    beta_all = jax.nn.sigmoid(small)
    causal, strict = _causal_masks()
    norm_w = normw_ref[...]

    for ci in range(rows // CHUNK):
        r0 = ci * CHUNK
        gcum = _chunk_cumsum(g_all[r0:r0 + CHUNK, :])
        gcum_t = gcum.T
        for h in range(GDN_HEADS):
            c0 = h * GDN_HEAD_DIM
            q = y[r0:r0 + CHUNK, c0:c0 + GDN_HEAD_DIM]
            k = y[r0:r0 + CHUNK, GDN_WIDTH + c0:GDN_WIDTH + c0 + GDN_HEAD_DIM]
            v = y[r0:r0 + CHUNK, 2 * GDN_WIDTH + c0:2 * GDN_WIDTH + c0 + GDN_HEAD_DIM]
            q = q * lax.rsqrt(jnp.sum(q * q, axis=-1, keepdims=True) + RMS_EPS) * (GDN_HEAD_DIM ** -0.5)
            k = k * lax.rsqrt(jnp.sum(k * k, axis=-1, keepdims=True) + RMS_EPS)
            beta = beta_all[r0:r0 + CHUNK, SMALL_B + h:SMALL_B + h + 1]
            gc = gcum[:, SMALL_A + h:SMALL_A + h + 1]
            gr = gcum_t[SMALL_A + h:SMALL_A + h + 1, :]
            g_last = gc[CHUNK - 1:CHUNK, :]
            decay = _decay_matrix(gc, gr, causal)
            kb = k * beta
            a_mat = jnp.where(strict, _bdot_nt(kb, k) * decay, 0.0)
            t_mat = _unit_lower_inverse(a_mat)
            e_gc = jnp.exp(gc)
            u = _bdot(t_mat, v * beta)
            w = _bdot(t_mat, kb * e_gc)
            qk = _bdot_nt(q, k) * decay
            q_dec = q * e_gc
            k_dec = k * jnp.exp(g_last - gc)
            state = state_ref[h]
            v_new = u - _bdot(w, state)
            o = _bdot(q_dec, state) + _bdot(qk, v_new)
            state_ref[h] = state * jnp.exp(g_last) + _bdot_tn(k_dec, v_new)
            o = o * lax.rsqrt(jnp.mean(o * o, axis=-1, keepdims=True) + RMS_EPS) * norm_w
            gate = gate_ref[r0:r0 + CHUNK, c0:c0 + GDN_HEAD_DIM]
            o_ref[r0:r0 + CHUNK, c0:c0 + GDN_HEAD_DIM] = (o * _silu(gate)).astype(o_ref.dtype)


def _gated_deltanet(proj, conv_w, a_log, dt_bias, norm_w, *, bsz, seq, rows=256):
    nt = seq // rows
    nexpa = jnp.zeros((1, LANES), F32).at[0, SMALL_A:SMALL_A + GDN_HEADS].set(-jnp.exp(a_log))
    dtb = jnp.zeros((1, LANES), F32).at[0, SMALL_A:SMALL_A + GDN_HEADS].set(dt_bias)

    def rowblk(width, col):
        return pl.BlockSpec((rows, width), lambda b, t: (b * nt + t, col // width))

    def const(shape):
        return pl.BlockSpec(shape, lambda b, t: (0,) * len(shape))

    return pl.pallas_call(
        functools.partial(_gdn_kernel, rows=rows),
        out_shape=jax.ShapeDtypeStruct((bsz * seq, GDN_WIDTH), BF16),
        grid=(bsz, nt),
        in_specs=[rowblk(3 * GDN_WIDTH, COL_QKV), rowblk(GDN_WIDTH, COL_GATE), rowblk(LANES, COL_SMALL),
                  const((SHORT_CONV, 3 * GDN_WIDTH)), const((1, LANES)), const((1, LANES)),
                  const((1, GDN_HEAD_DIM))],
        out_specs=pl.BlockSpec((rows, GDN_WIDTH), lambda b, t: (b * nt + t, 0)),
        scratch_shapes=[pltpu.VMEM((8 + rows, 3 * GDN_WIDTH), F32),
                        pltpu.VMEM((GDN_HEADS, GDN_HEAD_DIM, GDN_HEAD_DIM), F32)],
        compiler_params=_params(("parallel", "arbitrary")), name="gated_deltanet",
    )(proj, proj, proj, conv_w, nexpa, dtb, norm_w.reshape(1, GDN_HEAD_DIM))


def _swa_kernel(sinks_ref, q_ref, kc_ref, kp_ref, vc_ref, vp_ref, cos_ref, sin_ref, cosp_ref, sinp_ref,
                o_ref):
    n = pl.program_id(1)
    blk = SWA_WINDOW
    cos, sin = cos_ref[...], sin_ref[...]
    cosp, sinp = cosp_ref[...], sinp_ref[...]

    def rope(x, c, s):
        return x * c + pltpu.roll(x, SWA_HEAD_DIM // 2, axis=1) * s

    rows = SWA_REP * blk
    qi = lax.broadcasted_iota(jnp.int32, (rows, 2 * blk), 0) & (blk - 1)
    kj = lax.broadcasted_iota(jnp.int32, (rows, 2 * blk), 1)
    dist = qi + blk - kj
    in_window = jnp.where(dist >= 0, jnp.where(dist < SWA_WINDOW, 1, 0), 0)
    has_prev = jnp.where(kj >= blk, 1, jnp.where(n > 0, 1, 0))
    mask = (in_window * has_prev) > 0
    head_of_row = lax.broadcasted_iota(jnp.int32, (rows, 1), 0) >> 7

    for g in range(SWA_KV_HEADS):
        d0 = g * SWA_HEAD_DIM
        q_rows = [rope(q_ref[:, (g * SWA_REP + j) * SWA_HEAD_DIM:(g * SWA_REP + j + 1) * SWA_HEAD_DIM], cos, sin)
                  for j in range(SWA_REP)]
        qs = jnp.concatenate(q_rows, axis=0)
        kband = jnp.concatenate([rope(kp_ref[:, d0:d0 + SWA_HEAD_DIM], cosp, sinp),
                                 rope(kc_ref[:, d0:d0 + SWA_HEAD_DIM], cos, sin)], axis=0)
        vband = jnp.concatenate([vp_ref[:, d0:d0 + SWA_HEAD_DIM], vc_ref[:, d0:d0 + SWA_HEAD_DIM]], axis=0)
        s = _bdot_nt(qs, kband) * (SWA_HEAD_DIM ** -0.5)
        s = jnp.where(mask, s, -jnp.inf)
        sink = jnp.zeros((rows, 1), F32)
        for j in range(SWA_REP):
            sink = jnp.where(head_of_row == j, sinks_ref[g * SWA_REP + j], sink)
        m = jnp.maximum(jnp.max(s, axis=-1, keepdims=True), sink)
        p = jnp.exp(s - m)
        denom = jnp.sum(p, axis=-1, keepdims=True) + jnp.exp(sink - m)
        o = _bdot(p / denom, vband)
        for j in range(SWA_REP):
            c0 = (g * SWA_REP + j) * SWA_HEAD_DIM
            o_ref[:, c0:c0 + SWA_HEAD_DIM] = o[j * blk:(j + 1) * blk, :].astype(o_ref.dtype)


def _sliding_window_attention(proj, sinks, cos_full, sin_signed, *, bsz, seq):
    blk = SWA_WINDOW
    nb = seq // blk
    kvw = SWA_KV_HEADS * SWA_HEAD_DIM

    def cur(width, col):
        return pl.BlockSpec((blk, width), lambda b, n: (b * nb + n, col // width))

    def prev(width, col):
        return pl.BlockSpec((blk, width), lambda b, n: (b * nb + jnp.maximum(n - 1, 0), col // width))

    tab_cur = pl.BlockSpec((blk, SWA_HEAD_DIM), lambda b, n: (n, 0))
    tab_prev = pl.BlockSpec((blk, SWA_HEAD_DIM), lambda b, n: (jnp.maximum(n - 1, 0), 0))
    return pl.pallas_call(
        _swa_kernel, out_shape=jax.ShapeDtypeStruct((bsz * seq, BRANCH_WIDTH), BF16),
        grid=(bsz, nb),
        in_specs=[pl.BlockSpec(memory_space=pltpu.SMEM),
                  cur(BRANCH_WIDTH, COL_SWA_Q), cur(kvw, COL_SWA_K), prev(kvw, COL_SWA_K),
                  cur(kvw, COL_SWA_V), prev(kvw, COL_SWA_V), tab_cur, tab_cur, tab_prev, tab_prev],
        out_specs=pl.BlockSpec((blk, BRANCH_WIDTH), lambda b, n: (b * nb + n, 0)),
        compiler_params=_params(("parallel", "parallel")), name="sliding_window_attention",
    )(sinks, proj, proj, proj, proj, proj, cos_full, sin_signed, cos_full, sin_signed)


def _conf_kernel(a_ref, gt_ref, convw_ref, convb_ref, lng_ref, lnb_ref, pww_ref, pwb_ref, o_ref, pad_ref):
    t_first = pl.program_id(1) == 0
    h = a_ref[...] * jax.nn.sigmoid(gt_ref[...])
    h = _causal_conv(pad_ref, h, convw_ref, 32, t_first) + convb_ref[...]
    mu = jnp.mean(h, axis=-1, keepdims=True)
    c = h - mu
    var = jnp.mean(c * c, axis=-1, keepdims=True)
    h = _silu(c * lax.rsqrt(var + LN_EPS) * lng_ref[...] + lnb_ref[...])
    o_ref[...] = (jnp.dot(h.astype(BF16), pww_ref[...], preferred_element_type=F32)
                  + pwb_ref[...]).astype(o_ref.dtype)


def _conformer_conv(proj, conv_w, conv_b, ln_g, ln_b, pw_w, pw_b, *, bsz, seq, rows=256):
    nt = seq // rows
    w = CONF_WIDTH

    def rowblk(col):
        return pl.BlockSpec((rows, w), lambda b, t: (b * nt + t, col // w))

    def const(shape):
        return pl.BlockSpec(shape, lambda b, t: (0,) * len(shape))

    return pl.pallas_call(
        _conf_kernel, out_shape=jax.ShapeDtypeStruct((bsz * seq, w), BF16),
        grid=(bsz, nt),
        in_specs=[rowblk(COL_CONF), rowblk(COL_CONF + w), const((CONF_CONV_WIDTH, w)), const((1, w)),
                  const((1, w)), const((1, w)), const((w, w)), const((1, w))],
        out_specs=pl.BlockSpec((rows, w), lambda b, t: (b * nt + t, 0)),
        scratch_shapes=[pltpu.VMEM((32 + rows, w), F32)],
        compiler_params=_params(("parallel", "arbitrary")), name="conformer_conv",
    )(proj, proj, conv_w, conv_b.reshape(1, w), ln_g.reshape(1, w), ln_b.reshape(1, w),
      pw_w.astype(BF16), pw_b.reshape(1, w))


def _ssd_kernel(z_ref, xs_ref, bc_ref, small_ref, wx_ref, wbc_ref, bx_ref, bbc_ref, nexpa_ref, dtb_ref,
                dskip_ref, normw_ref, o_ref, xpad_ref, bcpad_ref, state_ref, y_ref, *, rows):
    t_first = pl.program_id(1) == 0

    @pl.when(t_first)
    def _():
        state_ref[...] = jnp.zeros(state_ref.shape, F32)

    xs = _silu(_causal_conv(xpad_ref, xs_ref[...], wx_ref, 8, t_first) + bx_ref[...])
    bc = _silu(_causal_conv(bcpad_ref, bc_ref[...], wbc_ref, 8, t_first) + bbc_ref[...])
    dt_all = _softplus(small_ref[...] + dtb_ref[...])
    a_all = nexpa_ref[...] * dt_all
    causal, _ = _causal_masks()
    gs = SSD_GROUPS * SSD_STATE

    for ci in range(rows // CHUNK):
        r0 = ci * CHUNK
        acs = _chunk_cumsum(a_all[r0:r0 + CHUNK, :])
        acs_t = acs.T
        for g in range(SSD_GROUPS):
            bm = bc[r0:r0 + CHUNK, g * SSD_STATE:(g + 1) * SSD_STATE]
            cm = bc[r0:r0 + CHUNK, gs + g * SSD_STATE:gs + (g + 1) * SSD_STATE]
            cb = _bdot_nt(cm, bm)
            for e in range(SSD_HEADS_PER_GROUP):
                hd = g * SSD_HEADS_PER_GROUP + e
                lane = SMALL_DT + hd
                ac = acs[:, lane:lane + 1]
                ar = acs_t[lane:lane + 1, :]
                a_last = ac[CHUNK - 1:CHUNK, :]
                seg = _decay_matrix(ac, ar, causal)
                x_h = xs[r0:r0 + CHUNK, hd * SSD_HEAD_DIM:(hd + 1) * SSD_HEAD_DIM]
                xdt = x_h * dt_all[r0:r0 + CHUNK, lane:lane + 1]
                y_diag = _bdot(cb * seg, xdt)
                states = _bdot_tn(xdt * jnp.exp(a_last - ac), bm)
                h_in = state_ref[hd]
                y_off = _bdot_nt(cm, h_in) * jnp.exp(ac)
                state_ref[hd] = h_in * jnp.exp(a_last) + states
                y_ref[r0:r0 + CHUNK, hd * SSD_HEAD_DIM:(hd + 1) * SSD_HEAD_DIM] = y_diag + y_off

    y = y_ref[...] + dskip_ref[...] * xs
    y = y * _silu(z_ref[...])
    y = y * lax.rsqrt(jnp.mean(y * y, axis=-1, keepdims=True) + RMS_EPS) * normw_ref[...]
    o_ref[...] = y.astype(o_ref.dtype)


def _mamba2_ssd(proj, conv_w, conv_b, a_log, dt_bias, d_skip, norm_w, *, bsz, seq, rows=256):
    nt = seq // rows
    w = SSD_WIDTH
    nexpa = jnp.zeros((1, LANES), F32).at[0, SMALL_DT:SMALL_DT + SSD_HEADS].set(-jnp.exp(a_log))
    dtb = jnp.zeros((1, LANES), F32).at[0, SMALL_DT:SMALL_DT + SSD_HEADS].set(dt_bias)
    d_full = jnp.repeat(d_skip, SSD_HEAD_DIM).reshape(1, w)

    def rowblk(width, col):
        return pl.BlockSpec((rows, width), lambda b, t: (b * nt + t, col // width))

    def const(shape):
        return pl.BlockSpec(shape, lambda b, t: (0,) * len(shape))

    return pl.pallas_call(
        functools.partial(_ssd_kernel, rows=rows),
        out_shape=jax.ShapeDtypeStruct((bsz * seq, w), BF16),
        grid=(bsz, nt),
        in_specs=[rowblk(w, COL_Z), rowblk(w, COL_XS), rowblk(SSD_BC_WIDTH, COL_BC), rowblk(LANES, COL_SMALL),
                  const((SHORT_CONV, w)), const((SHORT_CONV, SSD_BC_WIDTH)), const((1, w)),
                  const((1, SSD_BC_WIDTH)), const((1, LANES)), const((1, LANES)), const((1, w)), const((1, w))],
        out_specs=pl.BlockSpec((rows, w), lambda b, t: (b * nt + t, 0)),
        scratch_shapes=[pltpu.VMEM((8 + rows, w), F32), pltpu.VMEM((8 + rows, SSD_BC_WIDTH), F32),
                        pltpu.VMEM((SSD_HEADS, SSD_HEAD_DIM, SSD_STATE), F32),
                        pltpu.VMEM((rows, w), F32)],
        compiler_params=_params(("parallel", "arbitrary")), name="mamba2_ssd",
    )(proj, proj, proj, proj, conv_w[:, :w], conv_w[:, w:], conv_b[:w].reshape(1, w),
      conv_b[w:].reshape(1, SSD_BC_WIDTH), nexpa, dtb, d_full, norm_w.reshape(1, w))


def _rearranged_in_weights(w_in):
    o = _REF_OFFS

    def seg(i):
        return w_in[:, o[i]:o[i + 1]]

    xbc = seg(9)
    cols = [seg(0), seg(1), seg(4), seg(7), seg(8), xbc, seg(5), seg(6), seg(2), seg(3), seg(10)]
    used = sum(c.shape[1] for c in cols)
    cols.append(jnp.zeros((w_in.shape[0], PROJ_COLS - used), w_in.dtype))
    return jnp.concatenate(cols, axis=1).astype(BF16), seg(11).astype(BF16)


def _rotary_tables(seq):
    inv_freq = ROPE_THETA ** (-jnp.arange(0, SWA_HEAD_DIM, 2, dtype=F32) / SWA_HEAD_DIM)
    ang = jnp.arange(seq, dtype=F32)[:, None] * inv_freq[None, :]
    cos, sin = jnp.cos(ang), jnp.sin(ang)
    return jnp.concatenate([cos, cos], axis=1), jnp.concatenate([-sin, sin], axis=1)


def kernel(x, w_in, gdn_conv_w, gdn_a_log, gdn_dt_bias, gdn_norm_w, swa_sinks, conf_conv_w, conf_conv_b, conf_ln_g, conf_ln_b, conf_pw_w, conf_pw_b, ssd_conv_w, ssd_conv_b, ssd_a_log, ssd_dt_bias, ssd_d, ssd_norm_w, w_branch, w_out, ln1_g, ln1_b, w_gate_up, w_down, ln2_g, ln2_b):
    bsz, seq, d = x.shape
    m = bsz * seq
    cos_full, sin_signed = _rotary_tables(seq)
    xf = x.reshape(m, d)
    xb = xf.astype(BF16)
    for i in range(DEPTH):
        w_small, w_gates = _rearranged_in_weights(w_in[i])
        proj = _matmul(xb, w_small, tm=512, tn=1536, out_dtype=F32, name="in_proj")
        o_a = _gated_deltanet(proj, gdn_conv_w[i], gdn_a_log[i], gdn_dt_bias[i], gdn_norm_w[i],
                              bsz=bsz, seq=seq)
        o_b = _sliding_window_attention(proj, swa_sinks[i], cos_full, sin_signed, bsz=bsz, seq=seq)
        o_c = _conformer_conv(proj, conf_conv_w[i], conf_conv_b[i], conf_ln_g[i], conf_ln_b[i],
                              conf_pw_w[i], conf_pw_b[i], bsz=bsz, seq=seq)
        o_d = _mamba2_ssd(proj, ssd_conv_w[i], ssd_conv_b[i], ssd_a_log[i], ssd_dt_bias[i], ssd_d[i],
                          ssd_norm_w[i], bsz=bsz, seq=seq)
        merged = _merge(xb, (o_a, o_b, o_c, o_d), w_gates, w_branch[i].astype(BF16), tm=512, tn=256)
        s1 = _matmul(merged, w_out[i].astype(BF16), tm=1024, tn=512, out_dtype=F32, resid=xf, name="out_proj")
        xf, xb = _layer_norm(s1, ln1_g[i], ln1_b[i], tr=256)
        act = _swiglu(xb, w_gate_up[i].astype(BF16), tm=1024, tn=256)
        s2 = _matmul(act, w_down[i].astype(BF16), tm=256, tn=512, out_dtype=F32, resid=xf, name="down_proj")
        xf, xb = _layer_norm(s2, ln2_g[i], ln2_b[i], tr=256)
    return xf.reshape(bsz, seq, d)
```

```python
import functools

import jax
import jax.numpy as jnp
import numpy as np
from jax import lax
from jax.experimental import pallas as pl
from jax.experimental.pallas import tpu as pltpu

F32 = jnp.float32
BF16 = jnp.bfloat16

D_MODEL = 4096
DEPTH = 2
N_BRANCH = 4
BRANCH_WIDTH = D_MODEL // 4

GDN_HEAD_DIM = 128
GDN_HEADS = BRANCH_WIDTH // GDN_HEAD_DIM
GDN_WIDTH = GDN_HEADS * GDN_HEAD_DIM
CHUNK = 64
SHORT_CONV = 4

SWA_HEAD_DIM = 128
SWA_Q_HEADS = BRANCH_WIDTH // SWA_HEAD_DIM
SWA_KV_HEADS = SWA_Q_HEADS // 4
SWA_REP = SWA_Q_HEADS // SWA_KV_HEADS
SWA_WINDOW = 128
ROPE_THETA = 10000.0

CONF_WIDTH = BRANCH_WIDTH
CONF_CONV_WIDTH = 31

SSD_HEAD_DIM = 64
SSD_WIDTH = BRANCH_WIDTH
SSD_HEADS = SSD_WIDTH // SSD_HEAD_DIM
SSD_GROUPS = 2
SSD_HEADS_PER_GROUP = SSD_HEADS // SSD_GROUPS
SSD_STATE = 128
SSD_BC_WIDTH = 2 * SSD_GROUPS * SSD_STATE

FFN_HIDDEN = -(-8 * D_MODEL // (3 * 256)) * 256

ALPHA = (2.0 * DEPTH) ** 0.25
LN_EPS = 1e-5
RMS_EPS = 1e-6

LANES = 128
VMEM_LIMIT = 52 * 1024 * 1024

COL_QKV = 0
COL_GATE = COL_QKV + 3 * GDN_WIDTH
COL_SWA_Q = COL_GATE + GDN_WIDTH
COL_CONF = COL_SWA_Q + BRANCH_WIDTH
COL_Z = COL_CONF + 2 * CONF_WIDTH
COL_XS = COL_Z + SSD_WIDTH
COL_BC = COL_XS + SSD_WIDTH
COL_SWA_K = COL_BC + SSD_BC_WIDTH
COL_SWA_V = COL_SWA_K + SWA_KV_HEADS * SWA_HEAD_DIM
COL_SMALL = COL_SWA_V + SWA_KV_HEADS * SWA_HEAD_DIM
SMALL_A = 0
SMALL_B = GDN_HEADS
SMALL_DT = 2 * GDN_HEADS
PROJ_COLS = 10752

_REF_SPLITS = (3 * GDN_WIDTH, GDN_WIDTH, GDN_HEADS, GDN_HEADS, BRANCH_WIDTH, 256, 256,
               2 * CONF_WIDTH, SSD_WIDTH, SSD_WIDTH + SSD_BC_WIDTH, SSD_HEADS, N_BRANCH * D_MODEL)
_REF_OFFS = np.concatenate([[0], np.cumsum(_REF_SPLITS)]).tolist()


def _params(sem, vmem=VMEM_LIMIT):
    return pltpu.CompilerParams(dimension_semantics=sem, vmem_limit_bytes=vmem)


def _bdot(a, b):
    return jnp.dot(a.astype(BF16), b.astype(BF16), preferred_element_type=F32)


def _bdot_nt(a, b):
    return lax.dot_general(a.astype(BF16), b.astype(BF16), (((1,), (1,)), ((), ())),
                           preferred_element_type=F32)


def _bdot_tn(a, b):
    return lax.dot_general(a.astype(BF16), b.astype(BF16), (((0,), (0,)), ((), ())),
                           preferred_element_type=F32)


def _silu(x):
    return x * jax.nn.sigmoid(x)


def _softplus(x):
    return jnp.maximum(x, 0.0) + jnp.log(1.0 + jnp.exp(-jnp.abs(x)))


def _cast_weights_once(w_refs, wb_refs):
    @pl.when(pl.program_id(1) == 0)
    def _():
        for w_ref, wb_ref in zip(w_refs, wb_refs):
            wb_ref[...] = w_ref[...].astype(BF16)


def _mm_kernel(x_ref, w_ref, *rest, has_resid, cast_w):
    rest = list(rest)
    r_ref = rest.pop(0) if has_resid else None
    o_ref = rest.pop(0)
    if cast_w:
        wb_ref = rest.pop(0)
        _cast_weights_once([w_ref], [wb_ref])
        w_ref = wb_ref
    acc = jnp.dot(x_ref[...], w_ref[...], preferred_element_type=F32)
    if has_resid:
        acc = ALPHA * r_ref[...] + acc
    o_ref[...] = acc.astype(o_ref.dtype)


def _matmul(x, w, layer, *, tm, tn, out_dtype, resid=None, name):
    m, k = x.shape
    n = w.shape[2]
    cast_w = w.dtype != BF16
    in_specs = [pl.BlockSpec((tm, k), lambda j, i: (i, 0)),
                pl.BlockSpec((None, k, tn), lambda j, i: (layer, 0, j))]
    args = [x, w]
    if resid is not None:
        in_specs.append(pl.BlockSpec((tm, tn), lambda j, i: (i, j)))
        args.append(resid)
    return pl.pallas_call(
        functools.partial(_mm_kernel, has_resid=resid is not None, cast_w=cast_w),
        out_shape=jax.ShapeDtypeStruct((m, n), out_dtype),
        grid=(n // tn, m // tm), in_specs=in_specs,
        out_specs=pl.BlockSpec((tm, tn), lambda j, i: (i, j)),
        scratch_shapes=[pltpu.VMEM((k, tn), BF16)] if cast_w else [],
        compiler_params=_params(("parallel", "arbitrary" if cast_w else "parallel")), name=name)(*args)


def _swiglu_kernel(x_ref, wg_ref, wu_ref, o_ref, wgb_ref, wub_ref):
    _cast_weights_once([wg_ref, wu_ref], [wgb_ref, wub_ref])
    x = x_ref[...]
    g = jnp.dot(x, wgb_ref[...], preferred_element_type=F32)
    u = jnp.dot(x, wub_ref[...], preferred_element_type=F32)
    o_ref[...] = (_silu(g) * u).astype(o_ref.dtype)


def _swiglu(x, w_gate_up, layer, *, tm, tn):
    m, k = x.shape
    hidden = w_gate_up.shape[2] // 2
    nb = hidden // tn
    return pl.pallas_call(
        _swiglu_kernel, out_shape=jax.ShapeDtypeStruct((m, hidden), BF16),
        grid=(nb, m // tm),
        in_specs=[pl.BlockSpec((tm, k), lambda j, i: (i, 0)),
                  pl.BlockSpec((None, k, tn), lambda j, i: (layer, 0, j)),
                  pl.BlockSpec((None, k, tn), lambda j, i: (layer, 0, j + nb))],
        out_specs=pl.BlockSpec((tm, tn), lambda j, i: (i, j)),
        scratch_shapes=[pltpu.VMEM((k, tn), BF16), pltpu.VMEM((k, tn), BF16)],
        compiler_params=_params(("parallel", "arbitrary")), name="swiglu")(x, w_gate_up, w_gate_up)


def _merge_kernel(x_ref, oa_ref, ob_ref, oc_ref, od_ref,
                  wg0, wg1, wg2, wg3, wb0, wb1, wb2, wb3, o_ref, wbb0, wbb1, wbb2, wbb3):
    _cast_weights_once([wb0, wb1, wb2, wb3], [wbb0, wbb1, wbb2, wbb3])
    x = x_ref[...]
    acc = None
    for o_r, wg, wb in ((oa_ref, wg0, wbb0), (ob_ref, wg1, wbb1), (oc_ref, wg2, wbb2), (od_ref, wg3, wbb3)):
        gate = jax.nn.sigmoid(jnp.dot(x, wg[...], preferred_element_type=F32))
        up = jnp.dot(o_r[...], wb[...], preferred_element_type=F32)
        acc = gate * up if acc is None else acc + gate * up
    o_ref[...] = acc.astype(o_ref.dtype)


def _merge(x, branches, w_gates, w_branch, layer, *, tm, tn):
    m, k = x.shape
    d = w_branch.shape[3]
    nb = d // tn
    bw = w_branch.shape[2]
    in_specs = [pl.BlockSpec((tm, k), lambda j, i: (i, 0))]
    in_specs += [pl.BlockSpec((tm, bw), lambda j, i: (i, 0)) for _ in range(N_BRANCH)]
    in_specs += [pl.BlockSpec((None, k, tn), functools.partial(lambda j, i, g: (layer, 0, g * nb + j), g=g))
                 for g in range(N_BRANCH)]
    in_specs += [pl.BlockSpec((None, None, bw, tn), functools.partial(lambda j, i, g: (layer, g, 0, j), g=g))
                 for g in range(N_BRANCH)]
    return pl.pallas_call(
        _merge_kernel, out_shape=jax.ShapeDtypeStruct((m, d), BF16),
        grid=(nb, m // tm), in_specs=in_specs,
        out_specs=pl.BlockSpec((tm, tn), lambda j, i: (i, j)),
        scratch_shapes=[pltpu.VMEM((bw, tn), BF16) for _ in range(N_BRANCH)],
        compiler_params=_params(("parallel", "arbitrary")), name="merge",
    )(x, *branches, *([w_gates] * N_BRANCH), *([w_branch] * N_BRANCH))


def _ln_kernel(s_ref, g_ref, b_ref, of_ref, ob_ref):
    s = s_ref[...]
    mu = jnp.mean(s, axis=-1, keepdims=True)
    c = s - mu
    var = jnp.mean(c * c, axis=-1, keepdims=True)
    y = c * lax.rsqrt(var + LN_EPS) * g_ref[...] + b_ref[...]
    of_ref[...] = y
    ob_ref[...] = y.astype(BF16)


def _layer_norm(s, g, b, *, tr):
    m, d = s.shape
    row = pl.BlockSpec((tr, d), lambda i: (i, 0))
    vec = pl.BlockSpec((1, d), lambda i: (0, 0))
    return pl.pallas_call(
        _ln_kernel,
        out_shape=(jax.ShapeDtypeStruct((m, d), F32), jax.ShapeDtypeStruct((m, d), BF16)),
        grid=(m // tr,), in_specs=[row, vec, vec], out_specs=(row, row),
        compiler_params=_params(("parallel",)), name="layer_norm")(s, g.reshape(1, d), b.reshape(1, d))


def _causal_conv(pad_ref, x, w_ref, halo, t_first):
    rows = x.shape[0]
    k = w_ref.shape[0]

    @pl.when(t_first)
    def _():
        pad_ref[0:halo, :] = jnp.zeros((halo, x.shape[1]), F32)

    pad_ref[halo:halo + rows, :] = x
    acc = None
    for j in range(k):
        start = halo - (k - 1) + j
        term = w_ref[j:j + 1, :] * pad_ref[start:start + rows, :]
        acc = term if acc is None else acc + term
    pad_ref[0:halo, :] = x[rows - halo:rows, :]
    return acc


def _chunk_cumsum(x):
    r = lax.broadcasted_iota(jnp.int32, (CHUNK, CHUNK), 0)
    c = lax.broadcasted_iota(jnp.int32, (CHUNK, CHUNK), 1)
    tril = jnp.where(r >= c, 1.0, 0.0).astype(F32)
    return jnp.dot(tril, x, preferred_element_type=F32, precision=lax.Precision.HIGHEST)


def _bmm(a, b):
    return lax.dot_general(a.astype(BF16), b.astype(BF16), (((2,), (1,)), ((0,), (0,))),
                           preferred_element_type=F32)


def _bmm_nt(a, b):
    return lax.dot_general(a.astype(BF16), b.astype(BF16), (((2,), (2,)), ((0,), (0,))),
                           preferred_element_type=F32)


def _bmm_tn(a, b):
    return lax.dot_general(a.astype(BF16), b.astype(BF16), (((1,), (1,)), ((0,), (0,))),
                           preferred_element_type=F32)


def _unit_lower_inverse(a):
    r = lax.broadcasted_iota(jnp.int32, (1, CHUNK, CHUNK), 1)
    c = lax.broadcasted_iota(jnp.int32, (1, CHUNK, CHUNK), 2)
    same16 = (r >> 4) == (c >> 4)
    same32 = (r >> 5) == (c >> 5)
    eye = jnp.where(r == c, 1.0, 0.0).astype(F32)
    ad = jnp.where(same16, a, 0.0)
    o1 = jnp.where(same32, a, 0.0) - ad
    o2 = jnp.where(same32, 0.0, a)
    a2 = _bmm(ad, ad)
    d = eye - ad
    a4 = _bmm(a2, a2)
    d = d + _bmm(d, a2)
    a8 = _bmm(a4, a4)
    d = d + _bmm(d, a4)
    d = d + _bmm(d, a8)
    d = d - _bmm(d, _bmm(o1, d))
    d = d - _bmm(d, _bmm(o2, d))
    return d


def _causal_masks(lead=()):
    shape = lead + (CHUNK, CHUNK)
    r = lax.broadcasted_iota(jnp.int32, shape, len(lead))
    c = lax.broadcasted_iota(jnp.int32, shape, len(lead) + 1)
    return r >= c, r > c


def _decay_matrix(col, row, causal):
    return jnp.where(causal, jnp.exp(jnp.where(causal, col - row, 0.0)), 0.0)


def _gdn_kernel(qkv_ref, gate_ref, small_ref, convw_ref, nexpa_ref, dtb_ref, normw_ref,
                o_ref, pad_ref, state_ref, *, rows):
    t_first = pl.program_id(1) == 0

    @pl.when(t_first)
    def _():
        state_ref[...] = jnp.zeros(state_ref.shape, F32)

    y = _silu(_causal_conv(pad_ref, qkv_ref[...], convw_ref, 8, t_first))
    small = small_ref[...]
    g_all = nexpa_ref[...] * _softplus(small + dtb_ref[...])
    beta_all = jax.nn.sigmoid(small)
    causal, strict = _causal_masks((1,))
    nch = rows // CHUNK
    gcums = [_chunk_cumsum(g_all[ci * CHUNK:(ci + 1) * CHUNK, :]) for ci in range(nch)]
    gcums_t = [g.T for g in gcums]

    def stack(fn):
        return jnp.stack([fn(ci, h) for ci in range(nch) for h in range(GDN_HEADS)], axis=0)

    def head_cols(base):
        return stack(lambda ci, h: y[ci * CHUNK:(ci + 1) * CHUNK,
                                     base + h * GDN_HEAD_DIM:base + (h + 1) * GDN_HEAD_DIM])

    q, k, v = head_cols(0), head_cols(GDN_WIDTH), head_cols(2 * GDN_WIDTH)
    q = q * lax.rsqrt(jnp.sum(q * q, axis=-1, keepdims=True) + RMS_EPS) * (GDN_HEAD_DIM ** -0.5)
    k = k * lax.rsqrt(jnp.sum(k * k, axis=-1, keepdims=True) + RMS_EPS)
    beta = stack(lambda ci, h: beta_all[ci * CHUNK:(ci + 1) * CHUNK, SMALL_B + h:SMALL_B + h + 1])
    gc = stack(lambda ci, h: gcums[ci][:, SMALL_A + h:SMALL_A + h + 1])
    gr = stack(lambda ci, h: gcums_t[ci][SMALL_A + h:SMALL_A + h + 1, :])
    g_last = gc[:, CHUNK - 1:CHUNK, :]
    decay = _decay_matrix(gc, gr, causal)
    kb = k * beta
    t_mat = _unit_lower_inverse(jnp.where(strict, _bmm_nt(kb, k) * decay, 0.0))
    e_gc = jnp.exp(gc)
    u = _bmm(t_mat, v * beta)
    w = _bmm(t_mat, kb * e_gc)
    qk = _bmm_nt(q, k) * decay
    q_dec = q * e_gc
    k_dec = k * jnp.exp(g_last - gc)
    chunk_decay = jnp.exp(g_last)

    norm_w = normw_ref[...]
    state = state_ref[...]
    for ci in range(nch):
        sl = slice(ci * GDN_HEADS, (ci + 1) * GDN_HEADS)
        v_new = u[sl] - _bmm(w[sl], state)
        o = _bmm(q_dec[sl], state) + _bmm(qk[sl], v_new)
        state = state * chunk_decay[sl] + _bmm_tn(k_dec[sl], v_new)
        o = o * lax.rsqrt(jnp.mean(o * o, axis=-1, keepdims=True) + RMS_EPS) * norm_w
        for h in range(GDN_HEADS):
            c0 = h * GDN_HEAD_DIM
            gate = gate_ref[ci * CHUNK:(ci + 1) * CHUNK, c0:c0 + GDN_HEAD_DIM]
            o_ref[ci * CHUNK:(ci + 1) * CHUNK, c0:c0 + GDN_HEAD_DIM] = (o[h] * _silu(gate)).astype(o_ref.dtype)
    state_ref[...] = state


def _gated_deltanet(proj, conv_w, a_log, dt_bias, norm_w, *, bsz, seq, rows=256):
    nt = seq // rows
    nexpa = jnp.zeros((1, LANES), F32).at[0, SMALL_A:SMALL_A + GDN_HEADS].set(-jnp.exp(a_log))
    dtb = jnp.zeros((1, LANES), F32).at[0, SMALL_A:SMALL_A + GDN_HEADS].set(dt_bias)

    def rowblk(width, col):
        return pl.BlockSpec((rows, width), lambda b, t: (b * nt + t, col // width))

    def const(shape):
        return pl.BlockSpec(shape, lambda b, t: (0,) * len(shape))

    return pl.pallas_call(
        functools.partial(_gdn_kernel, rows=rows),
        out_shape=jax.ShapeDtypeStruct((bsz * seq, GDN_WIDTH), BF16),
        grid=(bsz, nt),
        in_specs=[rowblk(3 * GDN_WIDTH, COL_QKV), rowblk(GDN_WIDTH, COL_GATE), rowblk(LANES, COL_SMALL),
                  const((SHORT_CONV, 3 * GDN_WIDTH)), const((1, LANES)), const((1, LANES)),
                  const((1, GDN_HEAD_DIM))],
        out_specs=pl.BlockSpec((rows, GDN_WIDTH), lambda b, t: (b * nt + t, 0)),
        scratch_shapes=[pltpu.VMEM((8 + rows, 3 * GDN_WIDTH), F32),
                        pltpu.VMEM((GDN_HEADS, GDN_HEAD_DIM, GDN_HEAD_DIM), F32)],
        compiler_params=_params(("parallel", "arbitrary")), name="gated_deltanet",
    )(proj, proj, proj, conv_w, nexpa, dtb, norm_w.reshape(1, GDN_HEAD_DIM))


def _swa_kernel(sinks_ref, q_ref, kc_ref, kp_ref, vc_ref, vp_ref, cos_ref, sin_ref, cosp_ref, sinp_ref,
                o_ref):
    n = pl.program_id(1)
    blk = SWA_WINDOW
    cos, sin = cos_ref[...], sin_ref[...]
    cosp, sinp = cosp_ref[...], sinp_ref[...]

    def rope(x, c, s):
        return x * c + pltpu.roll(x, SWA_HEAD_DIM // 2, axis=1) * s

    rows = SWA_REP * blk
    qi = lax.broadcasted_iota(jnp.int32, (rows, 2 * blk), 0) & (blk - 1)
    kj = lax.broadcasted_iota(jnp.int32, (rows, 2 * blk), 1)
    dist = qi + blk - kj
    in_window = jnp.where(dist >= 0, jnp.where(dist < SWA_WINDOW, 1, 0), 0)
    has_prev = jnp.where(kj >= blk, 1, jnp.where(n > 0, 1, 0))
    mask = (in_window * has_prev) > 0
    head_of_row = lax.broadcasted_iota(jnp.int32, (rows, 1), 0) >> 7

    for g in range(SWA_KV_HEADS):
        d0 = g * SWA_HEAD_DIM
        q_rows = [rope(q_ref[:, (g * SWA_REP + j) * SWA_HEAD_DIM:(g * SWA_REP + j + 1) * SWA_HEAD_DIM], cos, sin)
                  for j in range(SWA_REP)]
        qs = jnp.concatenate(q_rows, axis=0)
        kband = jnp.concatenate([rope(kp_ref[:, d0:d0 + SWA_HEAD_DIM], cosp, sinp),
                                 rope(kc_ref[:, d0:d0 + SWA_HEAD_DIM], cos, sin)], axis=0)
        vband = jnp.concatenate([vp_ref[:, d0:d0 + SWA_HEAD_DIM], vc_ref[:, d0:d0 + SWA_HEAD_DIM]], axis=0)
        s = _bdot_nt(qs, kband) * (SWA_HEAD_DIM ** -0.5)
        s = jnp.where(mask, s, -jnp.inf)
        sink = jnp.zeros((rows, 1), F32)
        for j in range(SWA_REP):
            sink = jnp.where(head_of_row == j, sinks_ref[g * SWA_REP + j], sink)
        m = jnp.maximum(jnp.max(s, axis=-1, keepdims=True), sink)
        p = jnp.exp(s - m)
        denom = jnp.sum(p, axis=-1, keepdims=True) + jnp.exp(sink - m)
        o = _bdot(p / denom, vband)
        for j in range(SWA_REP):
            c0 = (g * SWA_REP + j) * SWA_HEAD_DIM
            o_ref[:, c0:c0 + SWA_HEAD_DIM] = o[j * blk:(j + 1) * blk, :].astype(o_ref.dtype)


def _sliding_window_attention(proj, sinks, cos_full, sin_signed, *, bsz, seq):
    blk = SWA_WINDOW
    nb = seq // blk
    kvw = SWA_KV_HEADS * SWA_HEAD_DIM

    def cur(width, col):
        return pl.BlockSpec((blk, width), lambda b, n: (b * nb + n, col // width))

    def prev(width, col):
        return pl.BlockSpec((blk, width), lambda b, n: (b * nb + jnp.maximum(n - 1, 0), col // width))

    tab_cur = pl.BlockSpec((blk, SWA_HEAD_DIM), lambda b, n: (n, 0))
    tab_prev = pl.BlockSpec((blk, SWA_HEAD_DIM), lambda b, n: (jnp.maximum(n - 1, 0), 0))
    return pl.pallas_call(
        _swa_kernel, out_shape=jax.ShapeDtypeStruct((bsz * seq, BRANCH_WIDTH), BF16),
        grid=(bsz, nb),
        in_specs=[pl.BlockSpec(memory_space=pltpu.SMEM),
                  cur(BRANCH_WIDTH, COL_SWA_Q), cur(kvw, COL_SWA_K), prev(kvw, COL_SWA_K),
                  cur(kvw, COL_SWA_V), prev(kvw, COL_SWA_V), tab_cur, tab_cur, tab_prev, tab_prev],
        out_specs=pl.BlockSpec((blk, BRANCH_WIDTH), lambda b, n: (b * nb + n, 0)),
        compiler_params=_params(("parallel", "parallel")), name="sliding_window_attention",
    )(sinks, proj, proj, proj, proj, proj, cos_full, sin_signed, cos_full, sin_signed)


def _conf_kernel(a_ref, gt_ref, convw_ref, convb_ref, lng_ref, lnb_ref, pww_ref, pwb_ref, o_ref, pad_ref):
    t_first = pl.program_id(1) == 0
    h = a_ref[...] * jax.nn.sigmoid(gt_ref[...])
    h = _causal_conv(pad_ref, h, convw_ref, 32, t_first) + convb_ref[...]
    mu = jnp.mean(h, axis=-1, keepdims=True)
    c = h - mu
    var = jnp.mean(c * c, axis=-1, keepdims=True)
    h = _silu(c * lax.rsqrt(var + LN_EPS) * lng_ref[...] + lnb_ref[...])
    o_ref[...] = (jnp.dot(h.astype(BF16), pww_ref[...], preferred_element_type=F32)
                  + pwb_ref[...]).astype(o_ref.dtype)


def _conformer_conv(proj, conv_w, conv_b, ln_g, ln_b, pw_w, pw_b, *, bsz, seq, rows=256):
    nt = seq // rows
    w = CONF_WIDTH

    def rowblk(col):
        return pl.BlockSpec((rows, w), lambda b, t: (b * nt + t, col // w))

    def const(shape):
        return pl.BlockSpec(shape, lambda b, t: (0,) * len(shape))

    return pl.pallas_call(
        _conf_kernel, out_shape=jax.ShapeDtypeStruct((bsz * seq, w), BF16),
        grid=(bsz, nt),
        in_specs=[rowblk(COL_CONF), rowblk(COL_CONF + w), const((CONF_CONV_WIDTH, w)), const((1, w)),
                  const((1, w)), const((1, w)), const((w, w)), const((1, w))],
        out_specs=pl.BlockSpec((rows, w), lambda b, t: (b * nt + t, 0)),
        scratch_shapes=[pltpu.VMEM((32 + rows, w), F32)],
        compiler_params=_params(("parallel", "arbitrary")), name="conformer_conv",
    )(proj, proj, conv_w, conv_b.reshape(1, w), ln_g.reshape(1, w), ln_b.reshape(1, w),
      pw_w.astype(BF16), pw_b.reshape(1, w))


def _ssd_kernel(z_ref, xs_ref, bc_ref, small_ref, wx_ref, wbc_ref, bx_ref, bbc_ref, nexpa_ref, dtb_ref,
                dskip_ref, normw_ref, o_ref, xpad_ref, bcpad_ref, state_ref, y_ref, *, rows):
    t_first = pl.program_id(1) == 0

    @pl.when(t_first)
    def _():
        state_ref[...] = jnp.zeros(state_ref.shape, F32)

    xs = _silu(_causal_conv(xpad_ref, xs_ref[...], wx_ref, 8, t_first) + bx_ref[...])
    bc = _silu(_causal_conv(bcpad_ref, bc_ref[...], wbc_ref, 8, t_first) + bbc_ref[...])
    dt_all = _softplus(small_ref[...] + dtb_ref[...])
    a_all = nexpa_ref[...] * dt_all
    gs = SSD_GROUPS * SSD_STATE
    nch = rows // CHUNK
    pairs = SSD_HEADS // 2
    pairs_per_group = pairs // SSD_GROUPS

    er = lax.broadcasted_iota(jnp.int32, (LANES, SSD_WIDTH), 0)
    ec = lax.broadcasted_iota(jnp.int32, (LANES, SSD_WIDTH), 1)
    expand = jnp.where(er == SMALL_DT + (ec >> 6), 1.0, 0.0).astype(F32)

    def per_head_cols(v):
        return jnp.dot(v, expand, preferred_element_type=F32, precision=lax.Precision.HIGHEST)

    xdt = xs * per_head_cols(dt_all)
    lane = lax.broadcasted_iota(jnp.int32, (1, LANES), 1)
    low = lane < SSD_HEAD_DIM
    row2 = lax.broadcasted_iota(jnp.int32, (2 * CHUNK, 1), 0)
    l_idx = lax.broadcasted_iota(jnp.int32, (CHUNK, LANES), 0)
    s_idx = lax.broadcasted_iota(jnp.int32, (CHUNK, LANES), 1) & (CHUNK - 1)
    causal2 = l_idx >= s_idx

    lhs, rhs, xdec, bms, cms, e_ac, h_dec = [], [], [], [], [], [], []
    for ci in range(nch):
        r0 = ci * CHUNK
        acs = _chunk_cumsum(a_all[r0:r0 + CHUNK, :])
        ac_cols = per_head_cols(acs)
        acs_t = jnp.concatenate([acs, acs], axis=0).T
        a_last_cols = ac_cols[CHUNK - 1:CHUNK, :]
        xdec_c = xdt[r0:r0 + CHUNK, :] * jnp.exp(a_last_cols - ac_cols)
        e_ac_c = jnp.exp(ac_cols)
        cb2 = []
        for g in range(SSD_GROUPS):
            bm = bc[r0:r0 + CHUNK, g * SSD_STATE:(g + 1) * SSD_STATE]
            cm = bc[r0:r0 + CHUNK, gs + g * SSD_STATE:gs + (g + 1) * SSD_STATE]
            cb2.append(_bdot_nt(cm, jnp.concatenate([bm, bm], axis=0)))
        for p in range(pairs):
            g = p // pairs_per_group
            c0 = p * LANES
            l0 = SMALL_DT + 2 * p
            a_row = jnp.where(low, acs_t[l0:l0 + 1, :], acs_t[l0 + 1:l0 + 2, :])
            seg = _decay_matrix(ac_cols[:, c0:c0 + LANES], a_row, causal2)
            lhs.append(cb2[g] * seg)
            xp = xdt[r0:r0 + CHUNK, c0:c0 + LANES]
            rhs.append(jnp.concatenate([jnp.where(low, xp, 0.0), jnp.where(low, 0.0, xp)], axis=0))
            xdec.append(xdec_c[:, c0:c0 + LANES])
            bms.append(bc[r0:r0 + CHUNK, g * SSD_STATE:(g + 1) * SSD_STATE])
            cms.append(bc[r0:r0 + CHUNK, gs + g * SSD_STATE:gs + (g + 1) * SSD_STATE])
            e_ac.append(e_ac_c[:, c0:c0 + LANES])
            a_last = jnp.where(row2 < SSD_HEAD_DIM, acs[CHUNK - 1:CHUNK, l0:l0 + 1],
                               acs[CHUNK - 1:CHUNK, l0 + 1:l0 + 2])
            h_dec.append(jnp.exp(a_last))

    y_diag = _bmm(jnp.stack(lhs), jnp.stack(rhs))
    states = _bmm_tn(jnp.stack(xdec), jnp.stack(bms))
    cms, e_ac, h_dec = jnp.stack(cms), jnp.stack(e_ac), jnp.stack(h_dec)

    h = state_ref[...]
    for ci in range(nch):
        sl = slice(ci * pairs, (ci + 1) * pairs)
        y = y_diag[sl] + _bmm_nt(cms[sl], h) * e_ac[sl]
        h = h * h_dec[sl] + states[sl]
        for p in range(pairs):
            y_ref[ci * CHUNK:(ci + 1) * CHUNK, p * LANES:(p + 1) * LANES] = y[p]
    state_ref[...] = h

    y = y_ref[...] + dskip_ref[...] * xs
    y = y * _silu(z_ref[...])
    y = y * lax.rsqrt(jnp.mean(y * y, axis=-1, keepdims=True) + RMS_EPS) * normw_ref[...]
    o_ref[...] = y.astype(o_ref.dtype)


def _mamba2_ssd(proj, conv_w, conv_b, a_log, dt_bias, d_skip, norm_w, *, bsz, seq, rows=256):
    nt = seq // rows
    w = SSD_WIDTH
    nexpa = jnp.zeros((1, LANES), F32).at[0, SMALL_DT:SMALL_DT + SSD_HEADS].set(-jnp.exp(a_log))
    dtb = jnp.zeros((1, LANES), F32).at[0, SMALL_DT:SMALL_DT + SSD_HEADS].set(dt_bias)
    d_full = jnp.repeat(d_skip, SSD_HEAD_DIM).reshape(1, w)

    def rowblk(width, col):
        return pl.BlockSpec((rows, width), lambda b, t: (b * nt + t, col // width))

    def const(shape):
        return pl.BlockSpec(shape, lambda b, t: (0,) * len(shape))

    return pl.pallas_call(
        functools.partial(_ssd_kernel, rows=rows),
        out_shape=jax.ShapeDtypeStruct((bsz * seq, w), BF16),
        grid=(bsz, nt),
        in_specs=[rowblk(w, COL_Z), rowblk(w, COL_XS), rowblk(SSD_BC_WIDTH, COL_BC), rowblk(LANES, COL_SMALL),
                  const((SHORT_CONV, w)), const((SHORT_CONV, SSD_BC_WIDTH)), const((1, w)),
                  const((1, SSD_BC_WIDTH)), const((1, LANES)), const((1, LANES)), const((1, w)), const((1, w))],
        out_specs=pl.BlockSpec((rows, w), lambda b, t: (b * nt + t, 0)),
        scratch_shapes=[pltpu.VMEM((8 + rows, w), F32), pltpu.VMEM((8 + rows, SSD_BC_WIDTH), F32),
                        pltpu.VMEM((SSD_HEADS // 2, 2 * SSD_HEAD_DIM, SSD_STATE), F32),
                        pltpu.VMEM((rows, w), F32)],
        compiler_params=_params(("parallel", "arbitrary")), name="mamba2_ssd",
    )(proj, proj, proj, proj, conv_w[:, :w], conv_w[:, w:], conv_b[:w].reshape(1, w),
      conv_b[w:].reshape(1, SSD_BC_WIDTH), nexpa, dtb, d_full, norm_w.reshape(1, w))


def _rearranged_in_weights(w_in):
    o = _REF_OFFS

    def seg(i):
        return w_in[:, :, o[i]:o[i + 1]]

    cols = [seg(0), seg(1), seg(4), seg(7), seg(8), seg(9), seg(5), seg(6), seg(2), seg(3), seg(10)]
    used = sum(c.shape[2] for c in cols)
    cols.append(jnp.zeros(w_in.shape[:2] + (PROJ_COLS - used,), w_in.dtype))
    return jnp.concatenate(cols, axis=2).astype(BF16), seg(11).astype(BF16)


def _rotary_tables(seq):
    inv_freq = ROPE_THETA ** (-jnp.arange(0, SWA_HEAD_DIM, 2, dtype=F32) / SWA_HEAD_DIM)
    ang = jnp.arange(seq, dtype=F32)[:, None] * inv_freq[None, :]
    cos, sin = jnp.cos(ang), jnp.sin(ang)
    return jnp.concatenate([cos, cos], axis=1), jnp.concatenate([-sin, sin], axis=1)


def kernel(x, w_in, gdn_conv_w, gdn_a_log, gdn_dt_bias, gdn_norm_w, swa_sinks, conf_conv_w, conf_conv_b, conf_ln_g, conf_ln_b, conf_pw_w, conf_pw_b, ssd_conv_w, ssd_conv_b, ssd_a_log, ssd_dt_bias, ssd_d, ssd_norm_w, w_branch, w_out, ln1_g, ln1_b, w_gate_up, w_down, ln2_g, ln2_b):
    bsz, seq, d = x.shape
    m = bsz * seq
    cos_full, sin_signed = _rotary_tables(seq)
    xf = x.reshape(m, d)
    xb = xf.astype(BF16)
    w_small, w_gates = _rearranged_in_weights(w_in)
    w_down_b = w_down.astype(BF16)
    for i in range(DEPTH):
        proj = _matmul(xb, w_small, i, tm=512, tn=1536, out_dtype=F32, name="in_proj")
        o_a = _gated_deltanet(proj, gdn_conv_w[i], gdn_a_log[i], gdn_dt_bias[i], gdn_norm_w[i],
                              bsz=bsz, seq=seq)
        o_b = _sliding_window_attention(proj, swa_sinks[i], cos_full, sin_signed, bsz=bsz, seq=seq)
        o_c = _conformer_conv(proj, conf_conv_w[i], conf_conv_b[i], conf_ln_g[i], conf_ln_b[i],
                              conf_pw_w[i], conf_pw_b[i], bsz=bsz, seq=seq)
        o_d = _mamba2_ssd(proj, ssd_conv_w[i], ssd_conv_b[i], ssd_a_log[i], ssd_dt_bias[i], ssd_d[i],
                          ssd_norm_w[i], bsz=bsz, seq=seq)
        merged = _merge(xb, (o_a, o_b, o_c, o_d), w_gates, w_branch, i, tm=512, tn=256)
        s1 = _matmul(merged, w_out, i, tm=1024, tn=512, out_dtype=F32, resid=xf, name="out_proj")
        xf, xb = _layer_norm(s1, ln1_g[i], ln1_b[i], tr=256)
        act = _swiglu(xb, w_gate_up, i, tm=1024, tn=256)
        s2 = _matmul(act, w_down_b, i, tm=256, tn=512, out_dtype=F32, resid=xf, name="down_proj")
        xf, xb = _layer_norm(s2, ln2_g[i], ln2_b[i], tr=256)
    return xf.reshape(bsz, seq, d)
```

```python
import functools

import jax
import jax.numpy as jnp
import numpy as np
from jax import lax
from jax.experimental import pallas as pl
from jax.experimental.pallas import tpu as pltpu

F32 = jnp.float32
BF16 = jnp.bfloat16

D_MODEL = 4096
DEPTH = 2
N_BRANCH = 4
BRANCH_WIDTH = D_MODEL // 4

GDN_HEAD_DIM = 128
GDN_HEADS = BRANCH_WIDTH // GDN_HEAD_DIM
GDN_WIDTH = GDN_HEADS * GDN_HEAD_DIM
CHUNK = 64
SHORT_CONV = 4

SWA_HEAD_DIM = 128
SWA_Q_HEADS = BRANCH_WIDTH // SWA_HEAD_DIM
SWA_KV_HEADS = SWA_Q_HEADS // 4
SWA_REP = SWA_Q_HEADS // SWA_KV_HEADS
SWA_WINDOW = 128
ROPE_THETA = 10000.0

CONF_WIDTH = BRANCH_WIDTH
CONF_CONV_WIDTH = 31

SSD_HEAD_DIM = 64
SSD_WIDTH = BRANCH_WIDTH
SSD_HEADS = SSD_WIDTH // SSD_HEAD_DIM
SSD_GROUPS = 2
SSD_HEADS_PER_GROUP = SSD_HEADS // SSD_GROUPS
SSD_STATE = 128
SSD_BC_WIDTH = 2 * SSD_GROUPS * SSD_STATE

FFN_HIDDEN = -(-8 * D_MODEL // (3 * 256)) * 256

ALPHA = (2.0 * DEPTH) ** 0.25
LN_EPS = 1e-5
RMS_EPS = 1e-6

LANES = 128
VMEM_LIMIT = 52 * 1024 * 1024

COL_QKV = 0
COL_GATE = COL_QKV + 3 * GDN_WIDTH
COL_SWA_Q = COL_GATE + GDN_WIDTH
COL_CONF = COL_SWA_Q + BRANCH_WIDTH
COL_Z = COL_CONF + 2 * CONF_WIDTH
COL_XS = COL_Z + SSD_WIDTH
COL_BC = COL_XS + SSD_WIDTH
COL_SWA_K = COL_BC + SSD_BC_WIDTH
COL_SWA_V = COL_SWA_K + SWA_KV_HEADS * SWA_HEAD_DIM
COL_SMALL = COL_SWA_V + SWA_KV_HEADS * SWA_HEAD_DIM
SMALL_A = 0
SMALL_B = GDN_HEADS
SMALL_DT = 2 * GDN_HEADS
PROJ_COLS = 10752

_REF_SPLITS = (3 * GDN_WIDTH, GDN_WIDTH, GDN_HEADS, GDN_HEADS, BRANCH_WIDTH, 256, 256,
               2 * CONF_WIDTH, SSD_WIDTH, SSD_WIDTH + SSD_BC_WIDTH, SSD_HEADS, N_BRANCH * D_MODEL)
_REF_OFFS = np.concatenate([[0], np.cumsum(_REF_SPLITS)]).tolist()


def _params(sem, vmem=VMEM_LIMIT):
    return pltpu.CompilerParams(dimension_semantics=sem, vmem_limit_bytes=vmem)


def _bdot(a, b):
    return jnp.dot(a.astype(BF16), b.astype(BF16), preferred_element_type=F32)


def _bdot_nt(a, b):
    return lax.dot_general(a.astype(BF16), b.astype(BF16), (((1,), (1,)), ((), ())),
                           preferred_element_type=F32)


def _bdot_tn(a, b):
    return lax.dot_general(a.astype(BF16), b.astype(BF16), (((0,), (0,)), ((), ())),
                           preferred_element_type=F32)


def _silu(x):
    return x * jax.nn.sigmoid(x)


def _softplus(x):
    return jnp.maximum(x, 0.0) + jnp.log(1.0 + jnp.exp(-jnp.abs(x)))


def _cast_weights_once(w_refs, wb_refs):
    @pl.when(pl.program_id(1) == 0)
    def _():
        for w_ref, wb_ref in zip(w_refs, wb_refs):
            wb_ref[...] = w_ref[...].astype(BF16)


def _mm_kernel(x_ref, w_ref, *rest, has_resid, cast_w):
    rest = list(rest)
    r_ref = rest.pop(0) if has_resid else None
    o_ref = rest.pop(0)
    if cast_w:
        wb_ref = rest.pop(0)
        _cast_weights_once([w_ref], [wb_ref])
        w_ref = wb_ref
    acc = jnp.dot(x_ref[...], w_ref[...], preferred_element_type=F32)
    if has_resid:
        acc = ALPHA * r_ref[...] + acc
    o_ref[...] = acc.astype(o_ref.dtype)


def _matmul(x, w, layer, *, tm, tn, out_dtype, resid=None, name):
    m, k = x.shape
    n = w.shape[2]
    cast_w = w.dtype != BF16
    in_specs = [pl.BlockSpec((tm, k), lambda j, i: (i, 0)),
                pl.BlockSpec((None, k, tn), lambda j, i: (layer, 0, j))]
    args = [x, w]
    if resid is not None:
        in_specs.append(pl.BlockSpec((tm, tn), lambda j, i: (i, j)))
        args.append(resid)
    return pl.pallas_call(
        functools.partial(_mm_kernel, has_resid=resid is not None, cast_w=cast_w),
        out_shape=jax.ShapeDtypeStruct((m, n), out_dtype),
        grid=(n // tn, m // tm), in_specs=in_specs,
        out_specs=pl.BlockSpec((tm, tn), lambda j, i: (i, j)),
        scratch_shapes=[pltpu.VMEM((k, tn), BF16)] if cast_w else [],
        compiler_params=_params(("parallel", "arbitrary" if cast_w else "parallel")), name=name)(*args)


def _swiglu_kernel(x_ref, wg_ref, wu_ref, o_ref, wgb_ref, wub_ref):
    _cast_weights_once([wg_ref, wu_ref], [wgb_ref, wub_ref])
    x = x_ref[...]
    g = jnp.dot(x, wgb_ref[...], preferred_element_type=F32)
    u = jnp.dot(x, wub_ref[...], preferred_element_type=F32)
    o_ref[...] = (_silu(g) * u).astype(o_ref.dtype)


def _swiglu(x, w_gate_up, layer, *, tm, tn):
    m, k = x.shape
    hidden = w_gate_up.shape[2] // 2
    nb = hidden // tn
    return pl.pallas_call(
        _swiglu_kernel, out_shape=jax.ShapeDtypeStruct((m, hidden), BF16),
        grid=(nb, m // tm),
        in_specs=[pl.BlockSpec((tm, k), lambda j, i: (i, 0)),
                  pl.BlockSpec((None, k, tn), lambda j, i: (layer, 0, j)),
                  pl.BlockSpec((None, k, tn), lambda j, i: (layer, 0, j + nb))],
        out_specs=pl.BlockSpec((tm, tn), lambda j, i: (i, j)),
        scratch_shapes=[pltpu.VMEM((k, tn), BF16), pltpu.VMEM((k, tn), BF16)],
        compiler_params=_params(("parallel", "arbitrary")), name="swiglu")(x, w_gate_up, w_gate_up)


def _merge_kernel(x_ref, oa_ref, ob_ref, oc_ref, od_ref,
                  wg0, wg1, wg2, wg3, wb0, wb1, wb2, wb3, o_ref, wbb0, wbb1, wbb2, wbb3):
    _cast_weights_once([wb0, wb1, wb2, wb3], [wbb0, wbb1, wbb2, wbb3])
    x = x_ref[...]
    acc = None
    for o_r, wg, wb in ((oa_ref, wg0, wbb0), (ob_ref, wg1, wbb1), (oc_ref, wg2, wbb2), (od_ref, wg3, wbb3)):
        gate = jax.nn.sigmoid(jnp.dot(x, wg[...], preferred_element_type=F32))
        up = jnp.dot(o_r[...], wb[...], preferred_element_type=F32)
        acc = gate * up if acc is None else acc + gate * up
    o_ref[...] = acc.astype(o_ref.dtype)


def _merge(x, branches, w_gates, w_branch, layer, *, tm, tn):
    m, k = x.shape
    d = w_branch.shape[3]
    nb = d // tn
    bw = w_branch.shape[2]
    in_specs = [pl.BlockSpec((tm, k), lambda j, i: (i, 0))]
    in_specs += [pl.BlockSpec((tm, bw), lambda j, i: (i, 0)) for _ in range(N_BRANCH)]
    in_specs += [pl.BlockSpec((None, k, tn), functools.partial(lambda j, i, g: (layer, 0, g * nb + j), g=g))
                 for g in range(N_BRANCH)]
    in_specs += [pl.BlockSpec((None, None, bw, tn), functools.partial(lambda j, i, g: (layer, g, 0, j), g=g))
                 for g in range(N_BRANCH)]
    return pl.pallas_call(
        _merge_kernel, out_shape=jax.ShapeDtypeStruct((m, d), BF16),
        grid=(nb, m // tm), in_specs=in_specs,
        out_specs=pl.BlockSpec((tm, tn), lambda j, i: (i, j)),
        scratch_shapes=[pltpu.VMEM((bw, tn), BF16) for _ in range(N_BRANCH)],
        compiler_params=_params(("parallel", "arbitrary")), name="merge",
    )(x, *branches, *([w_gates] * N_BRANCH), *([w_branch] * N_BRANCH))


def _ln_kernel(s_ref, g_ref, b_ref, of_ref, ob_ref):
    s = s_ref[...]
    mu = jnp.mean(s, axis=-1, keepdims=True)
    c = s - mu
    var = jnp.mean(c * c, axis=-1, keepdims=True)
    y = c * lax.rsqrt(var + LN_EPS) * g_ref[...] + b_ref[...]
    of_ref[...] = y
    ob_ref[...] = y.astype(BF16)


def _layer_norm(s, g, b, *, tr):
    m, d = s.shape
    row = pl.BlockSpec((tr, d), lambda i: (i, 0))
    vec = pl.BlockSpec((1, d), lambda i: (0, 0))
    return pl.pallas_call(
        _ln_kernel,
        out_shape=(jax.ShapeDtypeStruct((m, d), F32), jax.ShapeDtypeStruct((m, d), BF16)),
        grid=(m // tr,), in_specs=[row, vec, vec], out_specs=(row, row),
        compiler_params=_params(("parallel",)), name="layer_norm")(s, g.reshape(1, d), b.reshape(1, d))


def _causal_conv(pad_ref, x, w_ref, halo, t_first):
    rows = x.shape[0]
    k = w_ref.shape[0]

    @pl.when(t_first)
    def _():
        pad_ref[0:halo, :] = jnp.zeros((halo, x.shape[1]), F32)

    pad_ref[halo:halo + rows, :] = x
    acc = None
    for j in range(k):
        start = halo - (k - 1) + j
        term = w_ref[j:j + 1, :] * pad_ref[start:start + rows, :]
        acc = term if acc is None else acc + term
    pad_ref[0:halo, :] = x[rows - halo:rows, :]
    return acc


def _chunk_cumsum(x):
    r = lax.broadcasted_iota(jnp.int32, (CHUNK, CHUNK), 0)
    c = lax.broadcasted_iota(jnp.int32, (CHUNK, CHUNK), 1)
    tril = jnp.where(r >= c, 1.0, 0.0).astype(F32)
    return jnp.dot(tril, x, preferred_element_type=F32, precision=lax.Precision.HIGHEST)


def _bmm(a, b):
    return lax.dot_general(a.astype(BF16), b.astype(BF16), (((2,), (1,)), ((0,), (0,))),
                           preferred_element_type=F32)


def _bmm_nt(a, b):
    return lax.dot_general(a.astype(BF16), b.astype(BF16), (((2,), (2,)), ((0,), (0,))),
                           preferred_element_type=F32)


def _bmm_tn(a, b):
    return lax.dot_general(a.astype(BF16), b.astype(BF16), (((1,), (1,)), ((0,), (0,))),
                           preferred_element_type=F32)


def _unit_lower_inverse(a):
    r = lax.broadcasted_iota(jnp.int32, (1, CHUNK, CHUNK), 1)
    c = lax.broadcasted_iota(jnp.int32, (1, CHUNK, CHUNK), 2)
    same16 = (r >> 4) == (c >> 4)
    same32 = (r >> 5) == (c >> 5)
    eye = jnp.where(r == c, 1.0, 0.0).astype(F32)
    ad = jnp.where(same16, a, 0.0)
    o1 = jnp.where(same32, a, 0.0) - ad
    o2 = jnp.where(same32, 0.0, a)
    a2 = _bmm(ad, ad)
    d = eye - ad
    a4 = _bmm(a2, a2)
    d = d + _bmm(d, a2)
    a8 = _bmm(a4, a4)
    d = d + _bmm(d, a4)
    d = d + _bmm(d, a8)
    d = d - _bmm(d, _bmm(o1, d))
    d = d - _bmm(d, _bmm(o2, d))
    return d


def _causal_masks(lead=()):
    shape = lead + (CHUNK, CHUNK)
    r = lax.broadcasted_iota(jnp.int32, shape, len(lead))
    c = lax.broadcasted_iota(jnp.int32, shape, len(lead) + 1)
    return r >= c, r > c


def _decay_matrix(col, row, causal):
    return jnp.where(causal, jnp.exp(jnp.where(causal, col - row, 0.0)), 0.0)


def _gdn_kernel(qkv_ref, gate_ref, small_ref, convw_ref, nexpa_ref, dtb_ref, normw_ref,
                o_ref, pad_ref, state_ref, *, rows):
    t_first = pl.program_id(1) == 0

    @pl.when(t_first)
    def _():
        state_ref[...] = jnp.zeros(state_ref.shape, F32)

    y = _silu(_causal_conv(pad_ref, qkv_ref[...], convw_ref, 8, t_first))
    small = small_ref[...]
    g_all = nexpa_ref[...] * _softplus(small + dtb_ref[...])
    beta_all = jax.nn.sigmoid(small)
    causal, strict = _causal_masks((1,))
    nch = rows // CHUNK
    gcums = [_chunk_cumsum(g_all[ci * CHUNK:(ci + 1) * CHUNK, :]) for ci in range(nch)]
    gcums_t = [g.T for g in gcums]

    def stack(fn):
        return jnp.stack([fn(ci, h) for ci in range(nch) for h in range(GDN_HEADS)], axis=0)

    def head_cols(base):
        return stack(lambda ci, h: y[ci * CHUNK:(ci + 1) * CHUNK,
                                     base + h * GDN_HEAD_DIM:base + (h + 1) * GDN_HEAD_DIM])

    q, k, v = head_cols(0), head_cols(GDN_WIDTH), head_cols(2 * GDN_WIDTH)
    q = q * lax.rsqrt(jnp.sum(q * q, axis=-1, keepdims=True) + RMS_EPS) * (GDN_HEAD_DIM ** -0.5)
    k = k * lax.rsqrt(jnp.sum(k * k, axis=-1, keepdims=True) + RMS_EPS)
    beta = stack(lambda ci, h: beta_all[ci * CHUNK:(ci + 1) * CHUNK, SMALL_B + h:SMALL_B + h + 1])
    gc = stack(lambda ci, h: gcums[ci][:, SMALL_A + h:SMALL_A + h + 1])
    gr = stack(lambda ci, h: gcums_t[ci][SMALL_A + h:SMALL_A + h + 1, :])
    g_last = gc[:, CHUNK - 1:CHUNK, :]
    decay = _decay_matrix(gc, gr, causal)
    kb = k * beta
    t_mat = _unit_lower_inverse(jnp.where(strict, _bmm_nt(kb, k) * decay, 0.0))
    e_gc = jnp.exp(gc)
    u = _bmm(t_mat, v * beta)
    w = _bmm(t_mat, kb * e_gc)
    qk = _bmm_nt(q, k) * decay
    q_dec = q * e_gc
    k_dec = k * jnp.exp(g_last - gc)
    chunk_decay = jnp.exp(g_last)

    norm_w = normw_ref[...]
    state = state_ref[...]
    for ci in range(nch):
        sl = slice(ci * GDN_HEADS, (ci + 1) * GDN_HEADS)
        v_new = u[sl] - _bmm(w[sl], state)
        o = _bmm(q_dec[sl], state) + _bmm(qk[sl], v_new)
        state = state * chunk_decay[sl] + _bmm_tn(k_dec[sl], v_new)
        o = o * lax.rsqrt(jnp.mean(o * o, axis=-1, keepdims=True) + RMS_EPS) * norm_w
        for h in range(GDN_HEADS):
            c0 = h * GDN_HEAD_DIM
            gate = gate_ref[ci * CHUNK:(ci + 1) * CHUNK, c0:c0 + GDN_HEAD_DIM]
            o_ref[ci * CHUNK:(ci + 1) * CHUNK, c0:c0 + GDN_HEAD_DIM] = (o[h] * _silu(gate)).astype(o_ref.dtype)
    state_ref[...] = state


def _gated_deltanet(proj, conv_w, a_log, dt_bias, norm_w, *, bsz, seq, rows=256):
    nt = seq // rows
    nexpa = jnp.zeros((1, LANES), F32).at[0, SMALL_A:SMALL_A + GDN_HEADS].set(-jnp.exp(a_log))
    dtb = jnp.zeros((1, LANES), F32).at[0, SMALL_A:SMALL_A + GDN_HEADS].set(dt_bias)

    def rowblk(width, col):
        return pl.BlockSpec((rows, width), lambda b, t: (b * nt + t, col // width))

    def const(shape):
        return pl.BlockSpec(shape, lambda b, t: (0,) * len(shape))

    return pl.pallas_call(
        functools.partial(_gdn_kernel, rows=rows),
        out_shape=jax.ShapeDtypeStruct((bsz * seq, GDN_WIDTH), BF16),
        grid=(bsz, nt),
        in_specs=[rowblk(3 * GDN_WIDTH, COL_QKV), rowblk(GDN_WIDTH, COL_GATE), rowblk(LANES, COL_SMALL),
                  const((SHORT_CONV, 3 * GDN_WIDTH)), const((1, LANES)), const((1, LANES)),
                  const((1, GDN_HEAD_DIM))],
        out_specs=pl.BlockSpec((rows, GDN_WIDTH), lambda b, t: (b * nt + t, 0)),
        scratch_shapes=[pltpu.VMEM((8 + rows, 3 * GDN_WIDTH), F32),
                        pltpu.VMEM((GDN_HEADS, GDN_HEAD_DIM, GDN_HEAD_DIM), F32)],
        compiler_params=_params(("parallel", "arbitrary")), name="gated_deltanet",
    )(proj, proj, proj, conv_w, nexpa, dtb, norm_w.reshape(1, GDN_HEAD_DIM))


def _swa_kernel(sinks_ref, q_ref, kc_ref, kp_ref, vc_ref, vp_ref, cos_ref, sin_ref, cosp_ref, sinp_ref,
                o_ref):
    n = pl.program_id(1)
    blk = SWA_WINDOW
    cos, sin = cos_ref[...], sin_ref[...]
    cosp, sinp = cosp_ref[...], sinp_ref[...]

    def rope(x, c, s):
        return x * c + pltpu.roll(x, SWA_HEAD_DIM // 2, axis=1) * s

    rows = SWA_REP * blk
    qi = lax.broadcasted_iota(jnp.int32, (rows, 2 * blk), 0) & (blk - 1)
    kj = lax.broadcasted_iota(jnp.int32, (rows, 2 * blk), 1)
    dist = qi + blk - kj
    in_window = jnp.where(dist >= 0, jnp.where(dist < SWA_WINDOW, 1, 0), 0)
    has_prev = jnp.where(kj >= blk, 1, jnp.where(n > 0, 1, 0))
    mask = (in_window * has_prev) > 0
    head_of_row = lax.broadcasted_iota(jnp.int32, (rows, 1), 0) >> 7

    for g in range(SWA_KV_HEADS):
        d0 = g * SWA_HEAD_DIM
        q_rows = [rope(q_ref[:, (g * SWA_REP + j) * SWA_HEAD_DIM:(g * SWA_REP + j + 1) * SWA_HEAD_DIM], cos, sin)
                  for j in range(SWA_REP)]
        qs = jnp.concatenate(q_rows, axis=0)
        kband = jnp.concatenate([rope(kp_ref[:, d0:d0 + SWA_HEAD_DIM], cosp, sinp),
                                 rope(kc_ref[:, d0:d0 + SWA_HEAD_DIM], cos, sin)], axis=0)
        vband = jnp.concatenate([vp_ref[:, d0:d0 + SWA_HEAD_DIM], vc_ref[:, d0:d0 + SWA_HEAD_DIM]], axis=0)
        s = _bdot_nt(qs, kband) * (SWA_HEAD_DIM ** -0.5)
        s = jnp.where(mask, s, -jnp.inf)
        sink = jnp.zeros((rows, 1), F32)
        for j in range(SWA_REP):
            sink = jnp.where(head_of_row == j, sinks_ref[g * SWA_REP + j], sink)
        m = jnp.maximum(jnp.max(s, axis=-1, keepdims=True), sink)
        p = jnp.exp(s - m)
        denom = jnp.sum(p, axis=-1, keepdims=True) + jnp.exp(sink - m)
        o = _bdot(p / denom, vband)
        for j in range(SWA_REP):
            c0 = (g * SWA_REP + j) * SWA_HEAD_DIM
            o_ref[:, c0:c0 + SWA_HEAD_DIM] = o[j * blk:(j + 1) * blk, :].astype(o_ref.dtype)


def _sliding_window_attention(proj, sinks, cos_full, sin_signed, *, bsz, seq):
    blk = SWA_WINDOW
    nb = seq // blk
    kvw = SWA_KV_HEADS * SWA_HEAD_DIM

    def cur(width, col):
        return pl.BlockSpec((blk, width), lambda b, n: (b * nb + n, col // width))

    def prev(width, col):
        return pl.BlockSpec((blk, width), lambda b, n: (b * nb + jnp.maximum(n - 1, 0), col // width))

    tab_cur = pl.BlockSpec((blk, SWA_HEAD_DIM), lambda b, n: (n, 0))
    tab_prev = pl.BlockSpec((blk, SWA_HEAD_DIM), lambda b, n: (jnp.maximum(n - 1, 0), 0))
    return pl.pallas_call(
        _swa_kernel, out_shape=jax.ShapeDtypeStruct((bsz * seq, BRANCH_WIDTH), BF16),
        grid=(bsz, nb),
        in_specs=[pl.BlockSpec(memory_space=pltpu.SMEM),
                  cur(BRANCH_WIDTH, COL_SWA_Q), cur(kvw, COL_SWA_K), prev(kvw, COL_SWA_K),
                  cur(kvw, COL_SWA_V), prev(kvw, COL_SWA_V), tab_cur, tab_cur, tab_prev, tab_prev],
        out_specs=pl.BlockSpec((blk, BRANCH_WIDTH), lambda b, n: (b * nb + n, 0)),
        compiler_params=_params(("parallel", "parallel")), name="sliding_window_attention",
    )(sinks, proj, proj, proj, proj, proj, cos_full, sin_signed, cos_full, sin_signed)


def _conf_kernel(a_ref, gt_ref, convw_ref, convb_ref, lng_ref, lnb_ref, pww_ref, pwb_ref, o_ref, pad_ref):
    t_first = pl.program_id(1) == 0
    h = a_ref[...] * jax.nn.sigmoid(gt_ref[...])
    h = _causal_conv(pad_ref, h, convw_ref, 32, t_first) + convb_ref[...]
    mu = jnp.mean(h, axis=-1, keepdims=True)
    c = h - mu
    var = jnp.mean(c * c, axis=-1, keepdims=True)
    h = _silu(c * lax.rsqrt(var + LN_EPS) * lng_ref[...] + lnb_ref[...])
    o_ref[...] = (jnp.dot(h.astype(BF16), pww_ref[...], preferred_element_type=F32)
                  + pwb_ref[...]).astype(o_ref.dtype)


def _conformer_conv(proj, conv_w, conv_b, ln_g, ln_b, pw_w, pw_b, *, bsz, seq, rows=256):
    nt = seq // rows
    w = CONF_WIDTH

    def rowblk(col):
        return pl.BlockSpec((rows, w), lambda b, t: (b * nt + t, col // w))

    def const(shape):
        return pl.BlockSpec(shape, lambda b, t: (0,) * len(shape))

    return pl.pallas_call(
        _conf_kernel, out_shape=jax.ShapeDtypeStruct((bsz * seq, w), BF16),
        grid=(bsz, nt),
        in_specs=[rowblk(COL_CONF), rowblk(COL_CONF + w), const((CONF_CONV_WIDTH, w)), const((1, w)),
                  const((1, w)), const((1, w)), const((w, w)), const((1, w))],
        out_specs=pl.BlockSpec((rows, w), lambda b, t: (b * nt + t, 0)),
        scratch_shapes=[pltpu.VMEM((32 + rows, w), F32)],
        compiler_params=_params(("parallel", "arbitrary")), name="conformer_conv",
    )(proj, proj, conv_w, conv_b.reshape(1, w), ln_g.reshape(1, w), ln_b.reshape(1, w),
      pw_w.astype(BF16), pw_b.reshape(1, w))


def _ssd_kernel(z_ref, xs_ref, bc_ref, small_ref, wx_ref, wbc_ref, bx_ref, bbc_ref, nexpa_ref, dtb_ref,
                dskip_ref, normw_ref, o_ref, xpad_ref, bcpad_ref, state_ref, y_ref, *, rows):
    t_first = pl.program_id(1) == 0

    @pl.when(t_first)
    def _():
        state_ref[...] = jnp.zeros(state_ref.shape, F32)

    xs = _silu(_causal_conv(xpad_ref, xs_ref[...], wx_ref, 8, t_first) + bx_ref[...])
    bc = _silu(_causal_conv(bcpad_ref, bc_ref[...], wbc_ref, 8, t_first) + bbc_ref[...])
    dt_all = _softplus(small_ref[...] + dtb_ref[...])
    a_all = nexpa_ref[...] * dt_all
    gs = SSD_GROUPS * SSD_STATE
    nch = rows // CHUNK
    pairs = SSD_HEADS // 2
    pairs_per_group = pairs // SSD_GROUPS

    er = lax.broadcasted_iota(jnp.int32, (LANES, SSD_WIDTH), 0)
    ec = lax.broadcasted_iota(jnp.int32, (LANES, SSD_WIDTH), 1)
    expand = jnp.where(er == SMALL_DT + (ec >> 6), 1.0, 0.0).astype(F32)

    def per_head_cols(v):
        return jnp.dot(v, expand, preferred_element_type=F32, precision=lax.Precision.HIGHEST)

    xdt = xs * per_head_cols(dt_all)
    lane = lax.broadcasted_iota(jnp.int32, (1, LANES), 1)
    low = lane < SSD_HEAD_DIM
    row2 = lax.broadcasted_iota(jnp.int32, (2 * CHUNK, 1), 0)
    l_idx = lax.broadcasted_iota(jnp.int32, (CHUNK, LANES), 0)
    s_idx = lax.broadcasted_iota(jnp.int32, (CHUNK, LANES), 1) & (CHUNK - 1)
    causal2 = l_idx >= s_idx

    lhs, rhs, xdec, bms, cms, e_ac, h_dec = [], [], [], [], [], [], []
    for ci in range(nch):
        r0 = ci * CHUNK
        acs = _chunk_cumsum(a_all[r0:r0 + CHUNK, :])
        ac_cols = per_head_cols(acs)
        acs_t = jnp.concatenate([acs, acs], axis=0).T
        a_last_cols = ac_cols[CHUNK - 1:CHUNK, :]
        xdec_c = xdt[r0:r0 + CHUNK, :] * jnp.exp(a_last_cols - ac_cols)
        e_ac_c = jnp.exp(ac_cols)
        cb2 = []
        for g in range(SSD_GROUPS):
            bm = bc[r0:r0 + CHUNK, g * SSD_STATE:(g + 1) * SSD_STATE]
            cm = bc[r0:r0 + CHUNK, gs + g * SSD_STATE:gs + (g + 1) * SSD_STATE]
            cb2.append(_bdot_nt(cm, jnp.concatenate([bm, bm], axis=0)))
        for p in range(pairs):
            g = p // pairs_per_group
            c0 = p * LANES
            l0 = SMALL_DT + 2 * p
            a_row = jnp.where(low, acs_t[l0:l0 + 1, :], acs_t[l0 + 1:l0 + 2, :])
            seg = _decay_matrix(ac_cols[:, c0:c0 + LANES], a_row, causal2)
            lhs.append(cb2[g] * seg)
            xp = xdt[r0:r0 + CHUNK, c0:c0 + LANES]
            rhs.append(jnp.concatenate([jnp.where(low, xp, 0.0), jnp.where(low, 0.0, xp)], axis=0))
            xdec.append(xdec_c[:, c0:c0 + LANES])
            bms.append(bc[r0:r0 + CHUNK, g * SSD_STATE:(g + 1) * SSD_STATE])
            cms.append(bc[r0:r0 + CHUNK, gs + g * SSD_STATE:gs + (g + 1) * SSD_STATE])
            e_ac.append(e_ac_c[:, c0:c0 + LANES])
            a_last = jnp.where(row2 < SSD_HEAD_DIM, acs[CHUNK - 1:CHUNK, l0:l0 + 1],
                               acs[CHUNK - 1:CHUNK, l0 + 1:l0 + 2])
            h_dec.append(jnp.exp(a_last))

    y_diag = _bmm(jnp.stack(lhs), jnp.stack(rhs))
    states = _bmm_tn(jnp.stack(xdec), jnp.stack(bms))
    cms, e_ac, h_dec = jnp.stack(cms), jnp.stack(e_ac), jnp.stack(h_dec)

    h = state_ref[...]
    for ci in range(nch):
        sl = slice(ci * pairs, (ci + 1) * pairs)
        y = y_diag[sl] + _bmm_nt(cms[sl], h) * e_ac[sl]
        h = h * h_dec[sl] + states[sl]
        for p in range(pairs):
            y_ref[ci * CHUNK:(ci + 1) * CHUNK, p * LANES:(p + 1) * LANES] = y[p]
    state_ref[...] = h

    y = y_ref[...] + dskip_ref[...] * xs
    y = y * _silu(z_ref[...])
    y = y * lax.rsqrt(jnp.mean(y * y, axis=-1, keepdims=True) + RMS_EPS) * normw_ref[...]
    o_ref[...] = y.astype(o_ref.dtype)


def _mamba2_ssd(proj, conv_w, conv_b, a_log, dt_bias, d_skip, norm_w, *, bsz, seq, rows=256):
    nt = seq // rows
    w = SSD_WIDTH
    nexpa = jnp.zeros((1, LANES), F32).at[0, SMALL_DT:SMALL_DT + SSD_HEADS].set(-jnp.exp(a_log))
    dtb = jnp.zeros((1, LANES), F32).at[0, SMALL_DT:SMALL_DT + SSD_HEADS].set(dt_bias)
    d_full = jnp.repeat(d_skip, SSD_HEAD_DIM).reshape(1, w)

    def rowblk(width, col):
        return pl.BlockSpec((rows, width), lambda b, t: (b * nt + t, col // width))

    def const(shape):
        return pl.BlockSpec(shape, lambda b, t: (0,) * len(shape))

    return pl.pallas_call(
        functools.partial(_ssd_kernel, rows=rows),
        out_shape=jax.ShapeDtypeStruct((bsz * seq, w), BF16),
        grid=(bsz, nt),
        in_specs=[rowblk(w, COL_Z), rowblk(w, COL_XS), rowblk(SSD_BC_WIDTH, COL_BC), rowblk(LANES, COL_SMALL),
                  const((SHORT_CONV, w)), const((SHORT_CONV, SSD_BC_WIDTH)), const((1, w)),
                  const((1, SSD_BC_WIDTH)), const((1, LANES)), const((1, LANES)), const((1, w)), const((1, w))],
        out_specs=pl.BlockSpec((rows, w), lambda b, t: (b * nt + t, 0)),
        scratch_shapes=[pltpu.VMEM((8 + rows, w), F32), pltpu.VMEM((8 + rows, SSD_BC_WIDTH), F32),
                        pltpu.VMEM((SSD_HEADS // 2, 2 * SSD_HEAD_DIM, SSD_STATE), F32),
                        pltpu.VMEM((rows, w), F32)],
        compiler_params=_params(("parallel", "arbitrary")), name="mamba2_ssd",
    )(proj, proj, proj, proj, conv_w[:, :w], conv_w[:, w:], conv_b[:w].reshape(1, w),
      conv_b[w:].reshape(1, SSD_BC_WIDTH), nexpa, dtb, d_full, norm_w.reshape(1, w))


PREP_TN = 512


def _regroup_kernel(wt_ref, *rest, small_at):
    o_ref = rest[-1]
    j = pl.program_id(1)

    @pl.when(j != small_at)
    def _():
        o_ref[...] = wt_ref[0].T.astype(BF16)

    if len(rest) == 2:
        @pl.when(j == small_at)
        def _():
            used = SMALL_DT + SSD_HEADS
            pad = jnp.zeros((PREP_TN - used, wt_ref.shape[2]), F32)
            rows = jnp.concatenate([wt_ref[0, 0:SMALL_DT, :], rest[0][0], pad], axis=0)
            o_ref[...] = rows.T.astype(BF16)


def _regroup_weights(wt, n_blocks, row_start, *, small_at=-1, dt_start=None, name):
    layers, _, d = wt.shape
    in_specs = [pl.BlockSpec((pl.Element(1), pl.Element(PREP_TN), pl.Element(d)),
                             lambda l, j: (l, pl.multiple_of(row_start(j), 8), 0))]
    args = [wt]
    if dt_start is not None:
        in_specs.append(pl.BlockSpec((pl.Element(1), pl.Element(SSD_HEADS), pl.Element(d)),
                                     lambda l, j: (l, dt_start, 0)))
        args.append(wt)
    return pl.pallas_call(
        functools.partial(_regroup_kernel, small_at=small_at),
        out_shape=jax.ShapeDtypeStruct((layers, d, n_blocks * PREP_TN), BF16),
        grid=(layers, n_blocks), in_specs=in_specs,
        out_specs=pl.BlockSpec((None, d, PREP_TN), lambda l, j: (l, 0, j)),
        compiler_params=_params(("parallel", "parallel")), name=name)(*args)


def _rearranged_in_weights(w_in):
    o = _REF_OFFS
    wt = jnp.swapaxes(w_in, 1, 2)
    q0, conf0, kv, small = (c // PREP_TN for c in (COL_SWA_Q, COL_CONF, COL_SWA_K, COL_SMALL))
    assert o[2] == COL_SWA_Q and o[3] - o[2] == GDN_HEADS == SMALL_B and o[4] - o[2] == SMALL_DT

    def mixer_rows(j):
        c = j * PREP_TN
        return jnp.where(j < q0, c, jnp.where(j < conf0, c + (o[4] - COL_SWA_Q),
                         jnp.where(j < kv, c + (o[7] - COL_CONF), jnp.where(j < small, o[5], o[2]))))

    w_small = _regroup_weights(wt, PROJ_COLS // PREP_TN, mixer_rows, small_at=small, dt_start=o[10],
                               name="regroup_mixer_weights")
    w_gates = _regroup_weights(wt, N_BRANCH * D_MODEL // PREP_TN, lambda j: o[11] + j * PREP_TN,
                               name="regroup_gate_weights")
    return w_small, w_gates


def _rotary_tables(seq):
    inv_freq = ROPE_THETA ** (-jnp.arange(0, SWA_HEAD_DIM, 2, dtype=F32) / SWA_HEAD_DIM)
    ang = jnp.arange(seq, dtype=F32)[:, None] * inv_freq[None, :]
    cos, sin = jnp.cos(ang), jnp.sin(ang)
    return jnp.concatenate([cos, cos], axis=1), jnp.concatenate([-sin, sin], axis=1)


def kernel(x, w_in, gdn_conv_w, gdn_a_log, gdn_dt_bias, gdn_norm_w, swa_sinks, conf_conv_w, conf_conv_b, conf_ln_g, conf_ln_b, conf_pw_w, conf_pw_b, ssd_conv_w, ssd_conv_b, ssd_a_log, ssd_dt_bias, ssd_d, ssd_norm_w, w_branch, w_out, ln1_g, ln1_b, w_gate_up, w_down, ln2_g, ln2_b):
    bsz, seq, d = x.shape
    m = bsz * seq
    cos_full, sin_signed = _rotary_tables(seq)
    xf = x.reshape(m, d)
    xb = xf.astype(BF16)
    w_small, w_gates = _rearranged_in_weights(w_in)
    w_down_b = w_down.astype(BF16)
    for i in range(DEPTH):
        proj = _matmul(xb, w_small, i, tm=512, tn=1536, out_dtype=F32, name="in_proj")
        o_a = _gated_deltanet(proj, gdn_conv_w[i], gdn_a_log[i], gdn_dt_bias[i], gdn_norm_w[i],
                              bsz=bsz, seq=seq)
        o_b = _sliding_window_attention(proj, swa_sinks[i], cos_full, sin_signed, bsz=bsz, seq=seq)
        o_c = _conformer_conv(proj, conf_conv_w[i], conf_conv_b[i], conf_ln_g[i], conf_ln_b[i],
                              conf_pw_w[i], conf_pw_b[i], bsz=bsz, seq=seq)
        o_d = _mamba2_ssd(proj, ssd_conv_w[i], ssd_conv_b[i], ssd_a_log[i], ssd_dt_bias[i], ssd_d[i],
                          ssd_norm_w[i], bsz=bsz, seq=seq)
        merged = _merge(xb, (o_a, o_b, o_c, o_d), w_gates, w_branch, i, tm=512, tn=256)
        s1 = _matmul(merged, w_out, i, tm=1024, tn=512, out_dtype=F32, resid=xf, name="out_proj")
        xf, xb = _layer_norm(s1, ln1_g[i], ln1_b[i], tr=256)
        act = _swiglu(xb, w_gate_up, i, tm=1024, tn=256)
        s2 = _matmul(act, w_down_b, i, tm=256, tn=512, out_dtype=F32, resid=xf, name="down_proj")
        xf, xb = _layer_norm(s2, ln2_g[i], ln2_b[i], tr=256)
    return xf.reshape(bsz, seq, d)
```

```python
import functools

import jax
import jax.numpy as jnp
import numpy as np
from jax import lax
from jax.experimental import pallas as pl
from jax.experimental.pallas import tpu as pltpu

F32 = jnp.float32
BF16 = jnp.bfloat16

D_MODEL = 4096
DEPTH = 2
N_BRANCH = 4
BRANCH_WIDTH = D_MODEL // 4

GDN_HEAD_DIM = 128
GDN_HEADS = BRANCH_WIDTH // GDN_HEAD_DIM
GDN_WIDTH = GDN_HEADS * GDN_HEAD_DIM
CHUNK = 64
SHORT_CONV = 4

SWA_HEAD_DIM = 128
SWA_Q_HEADS = BRANCH_WIDTH // SWA_HEAD_DIM
SWA_KV_HEADS = SWA_Q_HEADS // 4
SWA_REP = SWA_Q_HEADS // SWA_KV_HEADS
SWA_WINDOW = 128
ROPE_THETA = 10000.0

CONF_WIDTH = BRANCH_WIDTH
CONF_CONV_WIDTH = 31

SSD_HEAD_DIM = 64
SSD_WIDTH = BRANCH_WIDTH
SSD_HEADS = SSD_WIDTH // SSD_HEAD_DIM
SSD_GROUPS = 2
SSD_HEADS_PER_GROUP = SSD_HEADS // SSD_GROUPS
SSD_STATE = 128
SSD_BC_WIDTH = 2 * SSD_GROUPS * SSD_STATE

FFN_HIDDEN = -(-8 * D_MODEL // (3 * 256)) * 256

ALPHA = (2.0 * DEPTH) ** 0.25
LN_EPS = 1e-5
RMS_EPS = 1e-6

LANES = 128
VMEM_LIMIT = 52 * 1024 * 1024

COL_QKV = 0
COL_GATE = COL_QKV + 3 * GDN_WIDTH
COL_SWA_Q = COL_GATE + GDN_WIDTH
COL_CONF = COL_SWA_Q + BRANCH_WIDTH
COL_Z = COL_CONF + 2 * CONF_WIDTH
COL_XS = COL_Z + SSD_WIDTH
COL_BC = COL_XS + SSD_WIDTH
COL_SWA_K = COL_BC + SSD_BC_WIDTH
COL_SWA_V = COL_SWA_K + SWA_KV_HEADS * SWA_HEAD_DIM
COL_SMALL = COL_SWA_V + SWA_KV_HEADS * SWA_HEAD_DIM
SMALL_A = 0
SMALL_B = GDN_HEADS
SMALL_DT = 2 * GDN_HEADS
PROJ_COLS = 10752

_REF_SPLITS = (3 * GDN_WIDTH, GDN_WIDTH, GDN_HEADS, GDN_HEADS, BRANCH_WIDTH, 256, 256,
               2 * CONF_WIDTH, SSD_WIDTH, SSD_WIDTH + SSD_BC_WIDTH, SSD_HEADS, N_BRANCH * D_MODEL)
_REF_OFFS = np.concatenate([[0], np.cumsum(_REF_SPLITS)]).tolist()


def _params(sem, vmem=VMEM_LIMIT):
    return pltpu.CompilerParams(dimension_semantics=sem, vmem_limit_bytes=vmem)


def _bdot(a, b):
    return jnp.dot(a.astype(BF16), b.astype(BF16), preferred_element_type=F32)


def _bdot_nt(a, b):
    return lax.dot_general(a.astype(BF16), b.astype(BF16), (((1,), (1,)), ((), ())),
                           preferred_element_type=F32)


def _bdot_tn(a, b):
    return lax.dot_general(a.astype(BF16), b.astype(BF16), (((0,), (0,)), ((), ())),
                           preferred_element_type=F32)


def _silu(x):
    return x * jax.nn.sigmoid(x)


def _softplus(x):
    return jnp.maximum(x, 0.0) + jnp.log(1.0 + jnp.exp(-jnp.abs(x)))


def _cast_weights_once(w_refs, wb_refs):
    @pl.when(pl.program_id(1) == 0)
    def _():
        for w_ref, wb_ref in zip(w_refs, wb_refs):
            wb_ref[...] = w_ref[...].astype(BF16)


def _mm_kernel(x_ref, w_ref, *rest, has_resid, cast_w):
    rest = list(rest)
    r_ref = rest.pop(0) if has_resid else None
    o_ref = rest.pop(0)
    if cast_w:
        wb_ref = rest.pop(0)
        _cast_weights_once([w_ref], [wb_ref])
        w_ref = wb_ref
    acc = jnp.dot(x_ref[...], w_ref[...], preferred_element_type=F32)
    if has_resid:
        acc = ALPHA * r_ref[...] + acc
    o_ref[...] = acc.astype(o_ref.dtype)


def _matmul(x, w, layer, *, tm, tn, out_dtype, resid=None, name):
    m, k = x.shape
    n = w.shape[2]
    cast_w = w.dtype != BF16
    in_specs = [pl.BlockSpec((tm, k), lambda j, i: (i, 0)),
                pl.BlockSpec((None, k, tn), lambda j, i: (layer, 0, j))]
    args = [x, w]
    if resid is not None:
        in_specs.append(pl.BlockSpec((tm, tn), lambda j, i: (i, j)))
        args.append(resid)
    return pl.pallas_call(
        functools.partial(_mm_kernel, has_resid=resid is not None, cast_w=cast_w),
        out_shape=jax.ShapeDtypeStruct((m, n), out_dtype),
        grid=(n // tn, m // tm), in_specs=in_specs,
        out_specs=pl.BlockSpec((tm, tn), lambda j, i: (i, j)),
        scratch_shapes=[pltpu.VMEM((k, tn), BF16)] if cast_w else [],
        compiler_params=_params(("parallel", "arbitrary" if cast_w else "parallel")), name=name)(*args)


def _swiglu_kernel(x_ref, wg0_ref, wg1_ref, wu0_ref, wu1_ref, o_ref, wgb_ref, wub_ref):
    half = wg0_ref.shape[0]
    _cast_weights_once([wg0_ref, wg1_ref, wu0_ref, wu1_ref],
                       [wgb_ref.at[0:half], wgb_ref.at[half:2 * half], wub_ref.at[0:half], wub_ref.at[half:2 * half]])
    x = x_ref[...]
    g = jnp.dot(x, wgb_ref[...], preferred_element_type=F32)
    u = jnp.dot(x, wub_ref[...], preferred_element_type=F32)
    o_ref[...] = (_silu(g) * u).astype(o_ref.dtype)


def _swiglu(x, w_gate_up, layer, *, tm, tn):
    m, k = x.shape
    hidden = w_gate_up.shape[2] // 2
    nb = hidden // tn

    def weight_half(which, up, order):
        def index(j, i):
            ahead = jnp.minimum(j + jnp.where(i > order, 1, 0), nb - 1)
            return (layer, which, ahead + (nb if up else 0))
        return pl.BlockSpec((None, k // 2, tn), index)

    return pl.pallas_call(
        _swiglu_kernel, out_shape=jax.ShapeDtypeStruct((m, hidden), BF16),
        grid=(nb, m // tm),
        in_specs=[pl.BlockSpec((tm, k), lambda j, i: (i, 0)),
                  weight_half(0, False, 0), weight_half(1, False, 1),
                  weight_half(0, True, 2), weight_half(1, True, 3)],
        out_specs=pl.BlockSpec((tm, tn), lambda j, i: (i, j)),
        scratch_shapes=[pltpu.VMEM((k, tn), BF16), pltpu.VMEM((k, tn), BF16)],
        compiler_params=_params(("arbitrary", "arbitrary")), name="swiglu",
    )(x, *([w_gate_up] * 4))


def _merge_kernel(x_ref, oa_ref, ob_ref, oc_ref, od_ref,
                  wg0, wg1, wg2, wg3, wb0, wb1, wb2, wb3, o_ref, wbb0, wbb1, wbb2, wbb3):
    _cast_weights_once([wb0, wb1, wb2, wb3], [wbb0, wbb1, wbb2, wbb3])
    x = x_ref[...]
    acc = None
    for o_r, wg, wb in ((oa_ref, wg0, wbb0), (ob_ref, wg1, wbb1), (oc_ref, wg2, wbb2), (od_ref, wg3, wbb3)):
        gate = jax.nn.sigmoid(jnp.dot(x, wg[...], preferred_element_type=F32))
        up = jnp.dot(o_r[...], wb[...], preferred_element_type=F32)
        acc = gate * up if acc is None else acc + gate * up
    o_ref[...] = acc.astype(o_ref.dtype)


def _merge(x, branches, w_gates, w_branch, layer, *, tm, tn):
    m, k = x.shape
    d = w_branch.shape[3]
    nb = d // tn
    bw = w_branch.shape[2]
    in_specs = [pl.BlockSpec((tm, k), lambda j, i: (i, 0))]
    in_specs += [pl.BlockSpec((tm, bw), lambda j, i: (i, 0)) for _ in range(N_BRANCH)]
    in_specs += [pl.BlockSpec((None, k, tn), functools.partial(lambda j, i, g: (layer, 0, g * nb + j), g=g))
                 for g in range(N_BRANCH)]
    in_specs += [pl.BlockSpec((None, None, bw, tn), functools.partial(lambda j, i, g: (layer, g, 0, j), g=g))
                 for g in range(N_BRANCH)]
    return pl.pallas_call(
        _merge_kernel, out_shape=jax.ShapeDtypeStruct((m, d), BF16),
        grid=(nb, m // tm), in_specs=in_specs,
        out_specs=pl.BlockSpec((tm, tn), lambda j, i: (i, j)),
        scratch_shapes=[pltpu.VMEM((bw, tn), BF16) for _ in range(N_BRANCH)],
        compiler_params=_params(("parallel", "arbitrary")), name="merge",
    )(x, *branches, *([w_gates] * N_BRANCH), *([w_branch] * N_BRANCH))


def _ln_kernel(s_ref, g_ref, b_ref, of_ref, ob_ref):
    s = s_ref[...]
    mu = jnp.mean(s, axis=-1, keepdims=True)
    c = s - mu
    var = jnp.mean(c * c, axis=-1, keepdims=True)
    y = c * lax.rsqrt(var + LN_EPS) * g_ref[...] + b_ref[...]
    of_ref[...] = y
    ob_ref[...] = y.astype(BF16)


def _layer_norm(s, g, b, *, tr):
    m, d = s.shape
    row = pl.BlockSpec((tr, d), lambda i: (i, 0))
    vec = pl.BlockSpec((1, d), lambda i: (0, 0))
    return pl.pallas_call(
        _ln_kernel,
        out_shape=(jax.ShapeDtypeStruct((m, d), F32), jax.ShapeDtypeStruct((m, d), BF16)),
        grid=(m // tr,), in_specs=[row, vec, vec], out_specs=(row, row),
        compiler_params=_params(("parallel",)), name="layer_norm")(s, g.reshape(1, d), b.reshape(1, d))


def _causal_conv(pad_ref, x, w_ref, halo, t_first, shift_ref=None):
    rows = x.shape[0]
    k = w_ref.shape[0]

    @pl.when(t_first)
    def _():
        pad_ref[0:halo, :] = jnp.zeros((halo, x.shape[1]), F32)

    pad_ref[halo:halo + rows, :] = x
    acc = None
    sublanes = 8
    first = halo - (k - 1)
    for phase in range(min(sublanes, k)):
        taps = list(range(phase, k, sublanes))
        lo = first + taps[0]
        if shift_ref is not None:
            span = rows + taps[-1] - taps[0]
            shift_ref[0:span, :] = pad_ref[lo:lo + span, :]
        for j in taps:
            if shift_ref is not None:
                window = shift_ref[j - taps[0]:j - taps[0] + rows, :]
            else:
                window = pad_ref[first + j:first + j + rows, :]
            term = w_ref[j:j + 1, :] * window
            acc = term if acc is None else acc + term
    pad_ref[0:halo, :] = x[rows - halo:rows, :]
    return acc


def _chunk_cumsum(x):
    r = lax.broadcasted_iota(jnp.int32, (CHUNK, CHUNK), 0)
    c = lax.broadcasted_iota(jnp.int32, (CHUNK, CHUNK), 1)
    tril = jnp.where(r >= c, 1.0, 0.0).astype(F32)
    return jnp.dot(tril, x, preferred_element_type=F32, precision=lax.Precision.HIGHEST)


def _bmm(a, b):
    return lax.dot_general(a.astype(BF16), b.astype(BF16), (((2,), (1,)), ((0,), (0,))),
                           preferred_element_type=F32)


def _bmm_nt(a, b):
    return lax.dot_general(a.astype(BF16), b.astype(BF16), (((2,), (2,)), ((0,), (0,))),
                           preferred_element_type=F32)


def _bmm_tn(a, b):
    return lax.dot_general(a.astype(BF16), b.astype(BF16), (((1,), (1,)), ((0,), (0,))),
                           preferred_element_type=F32)


def _unit_lower_inverse(a):
    r = lax.broadcasted_iota(jnp.int32, (1, CHUNK, CHUNK), 1)
    c = lax.broadcasted_iota(jnp.int32, (1, CHUNK, CHUNK), 2)
    same16 = (r >> 4) == (c >> 4)
    same32 = (r >> 5) == (c >> 5)
    eye = jnp.where(r == c, 1.0, 0.0).astype(F32)
    ad = jnp.where(same16, a, 0.0)
    o1 = jnp.where(same32, a, 0.0) - ad
    o2 = jnp.where(same32, 0.0, a)
    a2 = _bmm(ad, ad)
    d = eye - ad
    a4 = _bmm(a2, a2)
    d = d + _bmm(d, a2)
    a8 = _bmm(a4, a4)
    d = d + _bmm(d, a4)
    d = d + _bmm(d, a8)
    d = d - _bmm(d, _bmm(o1, d))
    d = d - _bmm(d, _bmm(o2, d))
    return d


def _causal_masks(lead=()):
    shape = lead + (CHUNK, CHUNK)
    r = lax.broadcasted_iota(jnp.int32, shape, len(lead))
    c = lax.broadcasted_iota(jnp.int32, shape, len(lead) + 1)
    return r >= c, r > c


def _decay_matrix(col, row, causal):
    return jnp.where(causal, jnp.exp(jnp.where(causal, col - row, 0.0)), 0.0)


def _gdn_kernel(qkv_ref, gate_ref, small_ref, convw_ref, nexpa_ref, dtb_ref, normw_ref,
                o_ref, pad_ref, state_ref, *, rows):
    t_first = pl.program_id(1) == 0

    @pl.when(t_first)
    def _():
        state_ref[...] = jnp.zeros(state_ref.shape, F32)

    y = _silu(_causal_conv(pad_ref, qkv_ref[...], convw_ref, 8, t_first))
    small = small_ref[...]
    g_all = nexpa_ref[...] * _softplus(small + dtb_ref[...])
    beta_all = jax.nn.sigmoid(small)
    causal, strict = _causal_masks((1,))
    nch = rows // CHUNK
    gcums = [_chunk_cumsum(g_all[ci * CHUNK:(ci + 1) * CHUNK, :]) for ci in range(nch)]
    gcums_t = [g.T for g in gcums]

    def stack(fn):
        return jnp.stack([fn(ci, h) for ci in range(nch) for h in range(GDN_HEADS)], axis=0)

    def head_cols(base):
        return stack(lambda ci, h: y[ci * CHUNK:(ci + 1) * CHUNK,
                                     base + h * GDN_HEAD_DIM:base + (h + 1) * GDN_HEAD_DIM])

    q, k, v = head_cols(0), head_cols(GDN_WIDTH), head_cols(2 * GDN_WIDTH)
    q = q * lax.rsqrt(jnp.sum(q * q, axis=-1, keepdims=True) + RMS_EPS) * (GDN_HEAD_DIM ** -0.5)
    k = k * lax.rsqrt(jnp.sum(k * k, axis=-1, keepdims=True) + RMS_EPS)
    beta = stack(lambda ci, h: beta_all[ci * CHUNK:(ci + 1) * CHUNK, SMALL_B + h:SMALL_B + h + 1])
    gc = stack(lambda ci, h: gcums[ci][:, SMALL_A + h:SMALL_A + h + 1])
    gr = stack(lambda ci, h: gcums_t[ci][SMALL_A + h:SMALL_A + h + 1, :])
    g_last = gc[:, CHUNK - 1:CHUNK, :]
    decay = _decay_matrix(gc, gr, causal)
    kb = k * beta
    t_mat = _unit_lower_inverse(jnp.where(strict, _bmm_nt(kb, k) * decay, 0.0))
    e_gc = jnp.exp(gc)
    u = _bmm(t_mat, v * beta)
    w = _bmm(t_mat, kb * e_gc)
    qk = _bmm_nt(q, k) * decay
    q_dec = q * e_gc
    k_dec = k * jnp.exp(g_last - gc)
    chunk_decay = jnp.exp(g_last)

    norm_w = normw_ref[...]
    state = state_ref[...]
    for ci in range(nch):
        sl = slice(ci * GDN_HEADS, (ci + 1) * GDN_HEADS)
        v_new = u[sl] - _bmm(w[sl], state)
        o = _bmm(q_dec[sl], state) + _bmm(qk[sl], v_new)
        state = state * chunk_decay[sl] + _bmm_tn(k_dec[sl], v_new)
        o = o * lax.rsqrt(jnp.mean(o * o, axis=-1, keepdims=True) + RMS_EPS) * norm_w
        for h in range(GDN_HEADS):
            c0 = h * GDN_HEAD_DIM
            gate = gate_ref[ci * CHUNK:(ci + 1) * CHUNK, c0:c0 + GDN_HEAD_DIM]
            o_ref[ci * CHUNK:(ci + 1) * CHUNK, c0:c0 + GDN_HEAD_DIM] = (o[h] * _silu(gate)).astype(o_ref.dtype)
    state_ref[...] = state


def _gated_deltanet(proj, conv_w, a_log, dt_bias, norm_w, *, bsz, seq, rows=256):
    nt = seq // rows
    nexpa = jnp.zeros((1, LANES), F32).at[0, SMALL_A:SMALL_A + GDN_HEADS].set(-jnp.exp(a_log))
    dtb = jnp.zeros((1, LANES), F32).at[0, SMALL_A:SMALL_A + GDN_HEADS].set(dt_bias)

    def rowblk(width, col):
        return pl.BlockSpec((rows, width), lambda b, t: (b * nt + t, col // width))

    def const(shape):
        return pl.BlockSpec(shape, lambda b, t: (0,) * len(shape))

    return pl.pallas_call(
        functools.partial(_gdn_kernel, rows=rows),
        out_shape=jax.ShapeDtypeStruct((bsz * seq, GDN_WIDTH), BF16),
        grid=(bsz, nt),
        in_specs=[rowblk(3 * GDN_WIDTH, COL_QKV), rowblk(GDN_WIDTH, COL_GATE), rowblk(LANES, COL_SMALL),
                  const((SHORT_CONV, 3 * GDN_WIDTH)), const((1, LANES)), const((1, LANES)),
                  const((1, GDN_HEAD_DIM))],
        out_specs=pl.BlockSpec((rows, GDN_WIDTH), lambda b, t: (b * nt + t, 0)),
        scratch_shapes=[pltpu.VMEM((8 + rows, 3 * GDN_WIDTH), F32),
                        pltpu.VMEM((GDN_HEADS, GDN_HEAD_DIM, GDN_HEAD_DIM), F32)],
        compiler_params=_params(("parallel", "arbitrary")), name="gated_deltanet",
    )(proj, proj, proj, conv_w, nexpa, dtb, norm_w.reshape(1, GDN_HEAD_DIM))


def _swa_kernel(sinks_ref, q_ref, kc_ref, kp_ref, vc_ref, vp_ref, cos_ref, sin_ref, cosp_ref, sinp_ref,
                o_ref):
    n = pl.program_id(1)
    blk = SWA_WINDOW
    cos, sin = cos_ref[...], sin_ref[...]
    cosp, sinp = cosp_ref[...], sinp_ref[...]

    def rope(x, c, s):
        return x * c + pltpu.roll(x, SWA_HEAD_DIM // 2, axis=1) * s

    rows = SWA_REP * blk
    qi = lax.broadcasted_iota(jnp.int32, (rows, 2 * blk), 0) & (blk - 1)
    kj = lax.broadcasted_iota(jnp.int32, (rows, 2 * blk), 1)
    dist = qi + blk - kj
    in_window = jnp.where(dist >= 0, jnp.where(dist < SWA_WINDOW, 1, 0), 0)
    has_prev = jnp.where(kj >= blk, 1, jnp.where(n > 0, 1, 0))
    mask = (in_window * has_prev) > 0
    head_of_row = lax.broadcasted_iota(jnp.int32, (rows, 1), 0) >> 7

    for g in range(SWA_KV_HEADS):
        d0 = g * SWA_HEAD_DIM
        q_rows = [rope(q_ref[:, (g * SWA_REP + j) * SWA_HEAD_DIM:(g * SWA_REP + j + 1) * SWA_HEAD_DIM], cos, sin)
                  for j in range(SWA_REP)]
        qs = jnp.concatenate(q_rows, axis=0)
        kband = jnp.concatenate([rope(kp_ref[:, d0:d0 + SWA_HEAD_DIM], cosp, sinp),
                                 rope(kc_ref[:, d0:d0 + SWA_HEAD_DIM], cos, sin)], axis=0)
        vband = jnp.concatenate([vp_ref[:, d0:d0 + SWA_HEAD_DIM], vc_ref[:, d0:d0 + SWA_HEAD_DIM]], axis=0)
        s = _bdot_nt(qs, kband) * (SWA_HEAD_DIM ** -0.5)
        s = jnp.where(mask, s, -jnp.inf)
        sink = jnp.zeros((rows, 1), F32)
        for j in range(SWA_REP):
            sink = jnp.where(head_of_row == j, sinks_ref[g * SWA_REP + j], sink)
        m = jnp.maximum(jnp.max(s, axis=-1, keepdims=True), sink)
        p = jnp.exp(s - m)
        denom = jnp.sum(p, axis=-1, keepdims=True) + jnp.exp(sink - m)
        o = _bdot(p / denom, vband)
        for j in range(SWA_REP):
            c0 = (g * SWA_REP + j) * SWA_HEAD_DIM
            o_ref[:, c0:c0 + SWA_HEAD_DIM] = o[j * blk:(j + 1) * blk, :].astype(o_ref.dtype)


def _sliding_window_attention(proj, sinks, cos_full, sin_signed, *, bsz, seq):
    blk = SWA_WINDOW
    nb = seq // blk
    kvw = SWA_KV_HEADS * SWA_HEAD_DIM

    def cur(width, col):
        return pl.BlockSpec((blk, width), lambda b, n: (b * nb + n, col // width))

    def prev(width, col):
        return pl.BlockSpec((blk, width), lambda b, n: (b * nb + jnp.maximum(n - 1, 0), col // width))

    tab_cur = pl.BlockSpec((blk, SWA_HEAD_DIM), lambda b, n: (n, 0))
    tab_prev = pl.BlockSpec((blk, SWA_HEAD_DIM), lambda b, n: (jnp.maximum(n - 1, 0), 0))
    return pl.pallas_call(
        _swa_kernel, out_shape=jax.ShapeDtypeStruct((bsz * seq, BRANCH_WIDTH), BF16),
        grid=(bsz, nb),
        in_specs=[pl.BlockSpec(memory_space=pltpu.SMEM),
                  cur(BRANCH_WIDTH, COL_SWA_Q), cur(kvw, COL_SWA_K), prev(kvw, COL_SWA_K),
                  cur(kvw, COL_SWA_V), prev(kvw, COL_SWA_V), tab_cur, tab_cur, tab_prev, tab_prev],
        out_specs=pl.BlockSpec((blk, BRANCH_WIDTH), lambda b, n: (b * nb + n, 0)),
        compiler_params=_params(("parallel", "parallel")), name="sliding_window_attention",
    )(sinks, proj, proj, proj, proj, proj, cos_full, sin_signed, cos_full, sin_signed)


def _conf_kernel(a_ref, gt_ref, convw_ref, convb_ref, lng_ref, lnb_ref, pww_ref, pwb_ref, o_ref, pad_ref,
                 shift_ref):
    t_first = pl.program_id(1) == 0
    h = a_ref[...] * jax.nn.sigmoid(gt_ref[...])
    h = _causal_conv(pad_ref, h, convw_ref, 32, t_first, shift_ref) + convb_ref[...]
    mu = jnp.mean(h, axis=-1, keepdims=True)
    c = h - mu
    var = jnp.mean(c * c, axis=-1, keepdims=True)
    h = _silu(c * lax.rsqrt(var + LN_EPS) * lng_ref[...] + lnb_ref[...])
    o_ref[...] = (jnp.dot(h.astype(BF16), pww_ref[...], preferred_element_type=F32)
                  + pwb_ref[...]).astype(o_ref.dtype)


def _conformer_conv(proj, conv_w, conv_b, ln_g, ln_b, pw_w, pw_b, *, bsz, seq, rows=256):
    nt = seq // rows
    w = CONF_WIDTH

    def rowblk(col):
        return pl.BlockSpec((rows, w), lambda b, t: (b * nt + t, col // w))

    def const(shape):
        return pl.BlockSpec(shape, lambda b, t: (0,) * len(shape))

    return pl.pallas_call(
        _conf_kernel, out_shape=jax.ShapeDtypeStruct((bsz * seq, w), BF16),
        grid=(bsz, nt),
        in_specs=[rowblk(COL_CONF), rowblk(COL_CONF + w), const((CONF_CONV_WIDTH, w)), const((1, w)),
                  const((1, w)), const((1, w)), const((w, w)), const((1, w))],
        out_specs=pl.BlockSpec((rows, w), lambda b, t: (b * nt + t, 0)),
        scratch_shapes=[pltpu.VMEM((32 + rows, w), F32),
                        pltpu.VMEM((rows + 8 * ((CONF_CONV_WIDTH - 1) // 8), w), F32)],
        compiler_params=_params(("parallel", "arbitrary")), name="conformer_conv",
    )(proj, proj, conv_w, conv_b.reshape(1, w), ln_g.reshape(1, w), ln_b.reshape(1, w),
      pw_w.astype(BF16), pw_b.reshape(1, w))


def _ssd_kernel(z_ref, xs_ref, bc_ref, small_ref, wx_ref, wbc_ref, bx_ref, bbc_ref, nexpa_ref, dtb_ref,
                dskip_ref, normw_ref, o_ref, xpad_ref, bcpad_ref, state_ref, y_ref, *, rows):
    t_first = pl.program_id(1) == 0

    @pl.when(t_first)
    def _():
        state_ref[...] = jnp.zeros(state_ref.shape, F32)

    xs = _silu(_causal_conv(xpad_ref, xs_ref[...], wx_ref, 8, t_first) + bx_ref[...])
    bc = _silu(_causal_conv(bcpad_ref, bc_ref[...], wbc_ref, 8, t_first) + bbc_ref[...])
    dt_all = _softplus(small_ref[...] + dtb_ref[...])
    a_all = nexpa_ref[...] * dt_all
    gs = SSD_GROUPS * SSD_STATE
    nch = rows // CHUNK
    pairs = SSD_HEADS // 2
    pairs_per_group = pairs // SSD_GROUPS

    er = lax.broadcasted_iota(jnp.int32, (LANES, SSD_WIDTH), 0)
    ec = lax.broadcasted_iota(jnp.int32, (LANES, SSD_WIDTH), 1)
    expand = jnp.where(er == SMALL_DT + (ec >> 6), 1.0, 0.0).astype(F32)

    def per_head_cols(v):
        return jnp.dot(v, expand, preferred_element_type=F32, precision=lax.Precision.HIGHEST)

    xdt = xs * per_head_cols(dt_all)
    lane = lax.broadcasted_iota(jnp.int32, (1, LANES), 1)
    low = lane < SSD_HEAD_DIM
    row2 = lax.broadcasted_iota(jnp.int32, (2 * CHUNK, 1), 0)
    l_idx = lax.broadcasted_iota(jnp.int32, (CHUNK, LANES), 0)
    s_idx = lax.broadcasted_iota(jnp.int32, (CHUNK, LANES), 1) & (CHUNK - 1)
    causal2 = l_idx >= s_idx

    lhs, rhs, xdec, bms, cms, e_ac, h_dec = [], [], [], [], [], [], []
    for ci in range(nch):
        r0 = ci * CHUNK
        acs = _chunk_cumsum(a_all[r0:r0 + CHUNK, :])
        ac_cols = per_head_cols(acs)
        acs_t = jnp.concatenate([acs, acs], axis=0).T
        a_last_cols = ac_cols[CHUNK - 1:CHUNK, :]
        xdec_c = xdt[r0:r0 + CHUNK, :] * jnp.exp(a_last_cols - ac_cols)
        e_ac_c = jnp.exp(ac_cols)
        cb2 = []
        for g in range(SSD_GROUPS):
            bm = bc[r0:r0 + CHUNK, g * SSD_STATE:(g + 1) * SSD_STATE]
            cm = bc[r0:r0 + CHUNK, gs + g * SSD_STATE:gs + (g + 1) * SSD_STATE]
            cb2.append(_bdot_nt(cm, jnp.concatenate([bm, bm], axis=0)))
        for p in range(pairs):
            g = p // pairs_per_group
            c0 = p * LANES
            l0 = SMALL_DT + 2 * p
            a_row = jnp.where(low, acs_t[l0:l0 + 1, :], acs_t[l0 + 1:l0 + 2, :])
            seg = _decay_matrix(ac_cols[:, c0:c0 + LANES], a_row, causal2)
            lhs.append(cb2[g] * seg)
            xp = xdt[r0:r0 + CHUNK, c0:c0 + LANES]
            rhs.append(jnp.concatenate([jnp.where(low, xp, 0.0), jnp.where(low, 0.0, xp)], axis=0))
            xdec.append(xdec_c[:, c0:c0 + LANES])
            bms.append(bc[r0:r0 + CHUNK, g * SSD_STATE:(g + 1) * SSD_STATE])
            cms.append(bc[r0:r0 + CHUNK, gs + g * SSD_STATE:gs + (g + 1) * SSD_STATE])
            e_ac.append(e_ac_c[:, c0:c0 + LANES])
            a_last = jnp.where(row2 < SSD_HEAD_DIM, acs[CHUNK - 1:CHUNK, l0:l0 + 1],
                               acs[CHUNK - 1:CHUNK, l0 + 1:l0 + 2])
            h_dec.append(jnp.exp(a_last))

    y_diag = _bmm(jnp.stack(lhs), jnp.stack(rhs))
    states = _bmm_tn(jnp.stack(xdec), jnp.stack(bms))
    cms, e_ac, h_dec = jnp.stack(cms), jnp.stack(e_ac), jnp.stack(h_dec)

    h = state_ref[...]
    for ci in range(nch):
        sl = slice(ci * pairs, (ci + 1) * pairs)
        y = y_diag[sl] + _bmm_nt(cms[sl], h) * e_ac[sl]
        h = h * h_dec[sl] + states[sl]
        for p in range(pairs):
            y_ref[ci * CHUNK:(ci + 1) * CHUNK, p * LANES:(p + 1) * LANES] = y[p]
    state_ref[...] = h

    y = y_ref[...] + dskip_ref[...] * xs
    y = y * _silu(z_ref[...])
    y = y * lax.rsqrt(jnp.mean(y * y, axis=-1, keepdims=True) + RMS_EPS) * normw_ref[...]
    o_ref[...] = y.astype(o_ref.dtype)


def _mamba2_ssd(proj, conv_w, conv_b, a_log, dt_bias, d_skip, norm_w, *, bsz, seq, rows=256):
    nt = seq // rows
    w = SSD_WIDTH
    nexpa = jnp.zeros((1, LANES), F32).at[0, SMALL_DT:SMALL_DT + SSD_HEADS].set(-jnp.exp(a_log))
    dtb = jnp.zeros((1, LANES), F32).at[0, SMALL_DT:SMALL_DT + SSD_HEADS].set(dt_bias)
    d_full = jnp.repeat(d_skip, SSD_HEAD_DIM).reshape(1, w)

    def rowblk(width, col):
        return pl.BlockSpec((rows, width), lambda b, t: (b * nt + t, col // width))

    def const(shape):
        return pl.BlockSpec(shape, lambda b, t: (0,) * len(shape))

    return pl.pallas_call(
        functools.partial(_ssd_kernel, rows=rows),
        out_shape=jax.ShapeDtypeStruct((bsz * seq, w), BF16),
        grid=(bsz, nt),
        in_specs=[rowblk(w, COL_Z), rowblk(w, COL_XS), rowblk(SSD_BC_WIDTH, COL_BC), rowblk(LANES, COL_SMALL),
                  const((SHORT_CONV, w)), const((SHORT_CONV, SSD_BC_WIDTH)), const((1, w)),
                  const((1, SSD_BC_WIDTH)), const((1, LANES)), const((1, LANES)), const((1, w)), const((1, w))],
        out_specs=pl.BlockSpec((rows, w), lambda b, t: (b * nt + t, 0)),
        scratch_shapes=[pltpu.VMEM((8 + rows, w), F32), pltpu.VMEM((8 + rows, SSD_BC_WIDTH), F32),
                        pltpu.VMEM((SSD_HEADS // 2, 2 * SSD_HEAD_DIM, SSD_STATE), F32),
                        pltpu.VMEM((rows, w), F32)],
        compiler_params=_params(("parallel", "arbitrary")), name="mamba2_ssd",
    )(proj, proj, proj, proj, conv_w[:, :w], conv_w[:, w:], conv_b[:w].reshape(1, w),
      conv_b[w:].reshape(1, SSD_BC_WIDTH), nexpa, dtb, d_full, norm_w.reshape(1, w))


PREP_TN = 512


def _regroup_kernel(wt_ref, *rest, small_at):
    o_ref = rest[-1]
    j = pl.program_id(1)

    @pl.when(j != small_at)
    def _():
        o_ref[...] = wt_ref[0].T.astype(BF16)

    if len(rest) == 2:
        @pl.when(j == small_at)
        def _():
            used = SMALL_DT + SSD_HEADS
            pad = jnp.zeros((PREP_TN - used, wt_ref.shape[2]), F32)
            rows = jnp.concatenate([wt_ref[0, 0:SMALL_DT, :], rest[0][0], pad], axis=0)
            o_ref[...] = rows.T.astype(BF16)


def _regroup_weights(wt, n_blocks, row_start, *, small_at=-1, dt_start=None, name):
    layers, _, d = wt.shape
    in_specs = [pl.BlockSpec((pl.Element(1), pl.Element(PREP_TN), pl.Element(d)),
                             lambda l, j: (l, pl.multiple_of(row_start(j), 8), 0))]
    args = [wt]
    if dt_start is not None:
        in_specs.append(pl.BlockSpec((pl.Element(1), pl.Element(SSD_HEADS), pl.Element(d)),
                                     lambda l, j: (l, dt_start, 0)))
        args.append(wt)
    return pl.pallas_call(
        functools.partial(_regroup_kernel, small_at=small_at),
        out_shape=jax.ShapeDtypeStruct((layers, d, n_blocks * PREP_TN), BF16),
        grid=(layers, n_blocks), in_specs=in_specs,
        out_specs=pl.BlockSpec((None, d, PREP_TN), lambda l, j: (l, 0, j)),
        compiler_params=_params(("parallel", "parallel")), name=name)(*args)


def _rearranged_in_weights(w_in):
    o = _REF_OFFS
    wt = jnp.swapaxes(w_in, 1, 2)
    q0, conf0, kv, small = (c // PREP_TN for c in (COL_SWA_Q, COL_CONF, COL_SWA_K, COL_SMALL))
    assert o[2] == COL_SWA_Q and o[3] - o[2] == GDN_HEADS == SMALL_B and o[4] - o[2] == SMALL_DT

    def mixer_rows(j):
        c = j * PREP_TN
        return jnp.where(j < q0, c, jnp.where(j < conf0, c + (o[4] - COL_SWA_Q),
                         jnp.where(j < kv, c + (o[7] - COL_CONF), jnp.where(j < small, o[5], o[2]))))

    w_small = _regroup_weights(wt, PROJ_COLS // PREP_TN, mixer_rows, small_at=small, dt_start=o[10],
                               name="regroup_mixer_weights")
    w_gates = _regroup_weights(wt, N_BRANCH * D_MODEL // PREP_TN, lambda j: o[11] + j * PREP_TN,
                               name="regroup_gate_weights")
    return w_small, w_gates


def _rotary_tables(seq):
    inv_freq = ROPE_THETA ** (-jnp.arange(0, SWA_HEAD_DIM, 2, dtype=F32) / SWA_HEAD_DIM)
    ang = jnp.arange(seq, dtype=F32)[:, None] * inv_freq[None, :]
    cos, sin = jnp.cos(ang), jnp.sin(ang)
    return jnp.concatenate([cos, cos], axis=1), jnp.concatenate([-sin, sin], axis=1)


def kernel(x, w_in, gdn_conv_w, gdn_a_log, gdn_dt_bias, gdn_norm_w, swa_sinks, conf_conv_w, conf_conv_b, conf_ln_g, conf_ln_b, conf_pw_w, conf_pw_b, ssd_conv_w, ssd_conv_b, ssd_a_log, ssd_dt_bias, ssd_d, ssd_norm_w, w_branch, w_out, ln1_g, ln1_b, w_gate_up, w_down, ln2_g, ln2_b):
    bsz, seq, d = x.shape
    m = bsz * seq
    cos_full, sin_signed = _rotary_tables(seq)
    xf = x.reshape(m, d)
    xb = xf.astype(BF16)
    w_small, w_gates = _rearranged_in_weights(w_in)
    w_down_b = w_down.astype(BF16)
    for i in range(DEPTH):
        proj = _matmul(xb, w_small, i, tm=512, tn=1536, out_dtype=F32, name="in_proj")
        o_a = _gated_deltanet(proj, gdn_conv_w[i], gdn_a_log[i], gdn_dt_bias[i], gdn_norm_w[i],
                              bsz=bsz, seq=seq)
        o_b = _sliding_window_attention(proj, swa_sinks[i], cos_full, sin_signed, bsz=bsz, seq=seq)
        o_c = _conformer_conv(proj, conf_conv_w[i], conf_conv_b[i], conf_ln_g[i], conf_ln_b[i],
                              conf_pw_w[i], conf_pw_b[i], bsz=bsz, seq=seq)
        o_d = _mamba2_ssd(proj, ssd_conv_w[i], ssd_conv_b[i], ssd_a_log[i], ssd_dt_bias[i], ssd_d[i],
                          ssd_norm_w[i], bsz=bsz, seq=seq)
        merged = _merge(xb, (o_a, o_b, o_c, o_d), w_gates, w_branch, i, tm=512, tn=256)
        s1 = _matmul(merged, w_out, i, tm=1024, tn=512, out_dtype=F32, resid=xf, name="out_proj")
        xf, xb = _layer_norm(s1, ln1_g[i], ln1_b[i], tr=256)
        act = _swiglu(xb, w_gate_up, i, tm=1024, tn=256)
        s2 = _matmul(act, w_down_b, i, tm=256, tn=512, out_dtype=F32, resid=xf, name="down_proj")
        xf, xb = _layer_norm(s2, ln2_g[i], ln2_b[i], tr=256)
    return xf.reshape(bsz, seq, d)
```

```python
import functools

import jax
import jax.numpy as jnp
import numpy as np
from jax import lax
from jax.experimental import pallas as pl
from jax.experimental.pallas import tpu as pltpu

F32 = jnp.float32
BF16 = jnp.bfloat16

D_MODEL = 4096
DEPTH = 2
N_BRANCH = 4
BRANCH_WIDTH = D_MODEL // 4

GDN_HEAD_DIM = 128
GDN_HEADS = BRANCH_WIDTH // GDN_HEAD_DIM
GDN_WIDTH = GDN_HEADS * GDN_HEAD_DIM
CHUNK = 64
SHORT_CONV = 4

SWA_HEAD_DIM = 128
SWA_Q_HEADS = BRANCH_WIDTH // SWA_HEAD_DIM
SWA_KV_HEADS = SWA_Q_HEADS // 4
SWA_REP = SWA_Q_HEADS // SWA_KV_HEADS
SWA_WINDOW = 128
ROPE_THETA = 10000.0

CONF_WIDTH = BRANCH_WIDTH
CONF_CONV_WIDTH = 31

SSD_HEAD_DIM = 64
SSD_WIDTH = BRANCH_WIDTH
SSD_HEADS = SSD_WIDTH // SSD_HEAD_DIM
SSD_GROUPS = 2
SSD_HEADS_PER_GROUP = SSD_HEADS // SSD_GROUPS
SSD_STATE = 128
SSD_BC_WIDTH = 2 * SSD_GROUPS * SSD_STATE

FFN_HIDDEN = -(-8 * D_MODEL // (3 * 256)) * 256

ALPHA = (2.0 * DEPTH) ** 0.25
LN_EPS = 1e-5
RMS_EPS = 1e-6

LANES = 128
VMEM_LIMIT = 52 * 1024 * 1024

COL_QKV = 0
COL_GATE = COL_QKV + 3 * GDN_WIDTH
COL_SWA_Q = COL_GATE + GDN_WIDTH
COL_CONF = COL_SWA_Q + BRANCH_WIDTH
COL_Z = COL_CONF + 2 * CONF_WIDTH
COL_XS = COL_Z + SSD_WIDTH
COL_BC = COL_XS + SSD_WIDTH
COL_SWA_K = COL_BC + SSD_BC_WIDTH
COL_SWA_V = COL_SWA_K + SWA_KV_HEADS * SWA_HEAD_DIM
COL_SMALL = COL_SWA_V + SWA_KV_HEADS * SWA_HEAD_DIM
SMALL_A = 0
SMALL_B = GDN_HEADS
SMALL_DT = 2 * GDN_HEADS
PROJ_COLS = 10752

_REF_SPLITS = (3 * GDN_WIDTH, GDN_WIDTH, GDN_HEADS, GDN_HEADS, BRANCH_WIDTH, 256, 256,
               2 * CONF_WIDTH, SSD_WIDTH, SSD_WIDTH + SSD_BC_WIDTH, SSD_HEADS, N_BRANCH * D_MODEL)
_REF_OFFS = np.concatenate([[0], np.cumsum(_REF_SPLITS)]).tolist()


def _params(sem, vmem=VMEM_LIMIT):
    return pltpu.CompilerParams(dimension_semantics=sem, vmem_limit_bytes=vmem)


def _bdot(a, b):
    return jnp.dot(a.astype(BF16), b.astype(BF16), preferred_element_type=F32)


def _bdot_nt(a, b):
    return lax.dot_general(a.astype(BF16), b.astype(BF16), (((1,), (1,)), ((), ())),
                           preferred_element_type=F32)


def _bdot_tn(a, b):
    return lax.dot_general(a.astype(BF16), b.astype(BF16), (((0,), (0,)), ((), ())),
                           preferred_element_type=F32)


def _silu(x):
    return x * jax.nn.sigmoid(x)


def _softplus(x):
    return jnp.maximum(x, 0.0) + jnp.log(1.0 + jnp.exp(-jnp.abs(x)))


def _cast_weights_once(w_refs, wb_refs):
    @pl.when(pl.program_id(1) == 0)
    def _():
        for w_ref, wb_ref in zip(w_refs, wb_refs):
            wb_ref[...] = w_ref[...].astype(BF16)


def _mm_kernel(x_ref, w_ref, *rest, has_resid, cast_w):
    rest = list(rest)
    r_ref = rest.pop(0) if has_resid else None
    o_ref = rest.pop(0)
    if cast_w:
        wb_ref = rest.pop(0)
        _cast_weights_once([w_ref], [wb_ref])
        w_ref = wb_ref
    acc = jnp.dot(x_ref[...], w_ref[...], preferred_element_type=F32)
    if has_resid:
        acc = ALPHA * r_ref[...] + acc
    o_ref[...] = acc.astype(o_ref.dtype)


def _matmul(x, w, layer, *, tm, tn, out_dtype, resid=None, name):
    m, k = x.shape
    n = w.shape[2]
    cast_w = w.dtype != BF16
    in_specs = [pl.BlockSpec((tm, k), lambda j, i: (i, 0)),
                pl.BlockSpec((None, k, tn), lambda j, i: (layer, 0, j))]
    args = [x, w]
    if resid is not None:
        in_specs.append(pl.BlockSpec((tm, tn), lambda j, i: (i, j)))
        args.append(resid)
    return pl.pallas_call(
        functools.partial(_mm_kernel, has_resid=resid is not None, cast_w=cast_w),
        out_shape=jax.ShapeDtypeStruct((m, n), out_dtype),
        grid=(n // tn, m // tm), in_specs=in_specs,
        out_specs=pl.BlockSpec((tm, tn), lambda j, i: (i, j)),
        scratch_shapes=[pltpu.VMEM((k, tn), BF16)] if cast_w else [],
        compiler_params=_params(("parallel", "arbitrary" if cast_w else "parallel")), name=name)(*args)


def _swiglu_kernel(x_ref, wg0_ref, wg1_ref, wu0_ref, wu1_ref, wd_ref, o_ref, wdb_ref, wgb_ref, wub_ref):
    half = wg0_ref.shape[0]
    _cast_weights_once([wg0_ref, wg1_ref, wu0_ref, wu1_ref],
                       [wgb_ref.at[0:half], wgb_ref.at[half:2 * half], wub_ref.at[0:half], wub_ref.at[half:2 * half]])
    x = x_ref[...]
    g = jnp.dot(x, wgb_ref[...], preferred_element_type=F32)
    u = jnp.dot(x, wub_ref[...], preferred_element_type=F32)
    o_ref[...] = (_silu(g) * u).astype(o_ref.dtype)
    wdb_ref[...] = wd_ref[...].astype(BF16)


def _swiglu(x, w_gate_up, w_down, layer, *, tm, tn):
    m, k = x.shape
    hidden = w_gate_up.shape[2] // 2
    nb = hidden // tn
    n_inner = m // tm
    d_out = w_down.shape[2]
    wd_rows = hidden // (nb * n_inner)
    assert wd_rows * nb * n_inner == hidden and wd_rows % 16 == 0

    def weight_half(which, up, order):
        def index(j, i):
            ahead = jnp.minimum(j + jnp.where(i > order, 1, 0), nb - 1)
            return (layer, which, ahead + (nb if up else 0))
        return pl.BlockSpec((None, k // 2, tn), index)

    return pl.pallas_call(
        _swiglu_kernel,
        out_shape=(jax.ShapeDtypeStruct((m, hidden), BF16), jax.ShapeDtypeStruct((hidden, d_out), BF16)),
        grid=(nb, n_inner),
        in_specs=[pl.BlockSpec((tm, k), lambda j, i: (i, 0)),
                  weight_half(0, False, 0), weight_half(1, False, 1),
                  weight_half(0, True, 2), weight_half(1, True, 3),
                  pl.BlockSpec((None, wd_rows, d_out), lambda j, i: (layer, j * n_inner + i, 0))],
        out_specs=(pl.BlockSpec((tm, tn), lambda j, i: (i, j)),
                   pl.BlockSpec((wd_rows, d_out), lambda j, i: (j * n_inner + i, 0))),
        scratch_shapes=[pltpu.VMEM((k, tn), BF16), pltpu.VMEM((k, tn), BF16)],
        compiler_params=_params(("arbitrary", "arbitrary")), name="swiglu",
    )(x, *([w_gate_up] * 4), w_down)


def _merge_kernel(x_ref, oa_ref, ob_ref, oc_ref, od_ref,
                  wg0, wg1, wg2, wg3, wb0, wb1, wb2, wb3, o_ref, wbb0, wbb1, wbb2, wbb3):
    _cast_weights_once([wb0, wb1, wb2, wb3], [wbb0, wbb1, wbb2, wbb3])
    x = x_ref[...]
    acc = None
    for o_r, wg, wb in ((oa_ref, wg0, wbb0), (ob_ref, wg1, wbb1), (oc_ref, wg2, wbb2), (od_ref, wg3, wbb3)):
        gate = jax.nn.sigmoid(jnp.dot(x, wg[...], preferred_element_type=F32))
        up = jnp.dot(o_r[...], wb[...], preferred_element_type=F32)
        acc = gate * up if acc is None else acc + gate * up
    o_ref[...] = acc.astype(o_ref.dtype)


def _merge(x, branches, w_gates, w_branch, layer, *, tm, tn):
    m, k = x.shape
    d = w_branch.shape[3]
    nb = d // tn
    bw = w_branch.shape[2]
    in_specs = [pl.BlockSpec((tm, k), lambda j, i: (i, 0))]
    in_specs += [pl.BlockSpec((tm, bw), lambda j, i: (i, 0)) for _ in range(N_BRANCH)]
    in_specs += [pl.BlockSpec((k, tn), functools.partial(lambda j, i, g: (0, g * nb + j), g=g))
                 for g in range(N_BRANCH)]
    in_specs += [pl.BlockSpec((None, None, bw, tn), functools.partial(lambda j, i, g: (layer, g, 0, j), g=g))
                 for g in range(N_BRANCH)]
    return pl.pallas_call(
        _merge_kernel, out_shape=jax.ShapeDtypeStruct((m, d), BF16),
        grid=(nb, m // tm), in_specs=in_specs,
        out_specs=pl.BlockSpec((tm, tn), lambda j, i: (i, j)),
        scratch_shapes=[pltpu.VMEM((bw, tn), BF16) for _ in range(N_BRANCH)],
        compiler_params=_params(("parallel", "arbitrary")), name="merge",
    )(x, *branches, *([w_gates] * N_BRANCH), *([w_branch] * N_BRANCH))


def _ln_kernel(s_ref, g_ref, b_ref, of_ref, ob_ref):
    s = s_ref[...]
    mu = jnp.mean(s, axis=-1, keepdims=True)
    c = s - mu
    var = jnp.mean(c * c, axis=-1, keepdims=True)
    y = c * lax.rsqrt(var + LN_EPS) * g_ref[...] + b_ref[...]
    of_ref[...] = y
    ob_ref[...] = y.astype(BF16)


def _layer_norm(s, g, b, *, tr):
    m, d = s.shape
    row = pl.BlockSpec((tr, d), lambda i: (i, 0))
    vec = pl.BlockSpec((1, d), lambda i: (0, 0))
    return pl.pallas_call(
        _ln_kernel,
        out_shape=(jax.ShapeDtypeStruct((m, d), F32), jax.ShapeDtypeStruct((m, d), BF16)),
        grid=(m // tr,), in_specs=[row, vec, vec], out_specs=(row, row),
        compiler_params=_params(("parallel",)), name="layer_norm")(s, g.reshape(1, d), b.reshape(1, d))


def _causal_conv(pad_ref, x, w_ref, halo, t_first, shift_ref=None):
    rows = x.shape[0]
    k = w_ref.shape[0]

    @pl.when(t_first)
    def _():
        pad_ref[0:halo, :] = jnp.zeros((halo, x.shape[1]), F32)

    pad_ref[halo:halo + rows, :] = x
    acc = None
    sublanes = 8
    first = halo - (k - 1)
    for phase in range(min(sublanes, k)):
        taps = list(range(phase, k, sublanes))
        lo = first + taps[0]
        if shift_ref is not None:
            span = rows + taps[-1] - taps[0]
            shift_ref[0:span, :] = pad_ref[lo:lo + span, :]
        for j in taps:
            if shift_ref is not None:
                window = shift_ref[j - taps[0]:j - taps[0] + rows, :]
            else:
                window = pad_ref[first + j:first + j + rows, :]
            term = w_ref[j:j + 1, :] * window
            acc = term if acc is None else acc + term
    pad_ref[0:halo, :] = x[rows - halo:rows, :]
    return acc


def _chunk_cumsum(x):
    r = lax.broadcasted_iota(jnp.int32, (CHUNK, CHUNK), 0)
    c = lax.broadcasted_iota(jnp.int32, (CHUNK, CHUNK), 1)
    tril = jnp.where(r >= c, 1.0, 0.0).astype(F32)
    return jnp.dot(tril, x, preferred_element_type=F32, precision=lax.Precision.HIGHEST)


def _bmm(a, b):
    return lax.dot_general(a.astype(BF16), b.astype(BF16), (((2,), (1,)), ((0,), (0,))),
                           preferred_element_type=F32)


def _bmm_nt(a, b):
    return lax.dot_general(a.astype(BF16), b.astype(BF16), (((2,), (2,)), ((0,), (0,))),
                           preferred_element_type=F32)


def _bmm_tn(a, b):
    return lax.dot_general(a.astype(BF16), b.astype(BF16), (((1,), (1,)), ((0,), (0,))),
                           preferred_element_type=F32)


def _unit_lower_inverse(a):
    r = lax.broadcasted_iota(jnp.int32, (1, CHUNK, CHUNK), 1)
    c = lax.broadcasted_iota(jnp.int32, (1, CHUNK, CHUNK), 2)
    same16 = (r >> 4) == (c >> 4)
    same32 = (r >> 5) == (c >> 5)
    eye = jnp.where(r == c, 1.0, 0.0).astype(F32)
    ad = jnp.where(same16, a, 0.0)
    o1 = jnp.where(same32, a, 0.0) - ad
    o2 = jnp.where(same32, 0.0, a)
    a2 = _bmm(ad, ad)
    d = eye - ad
    a4 = _bmm(a2, a2)
    d = d + _bmm(d, a2)
    a8 = _bmm(a4, a4)
    d = d + _bmm(d, a4)
    d = d + _bmm(d, a8)
    d = d - _bmm(d, _bmm(o1, d))
    d = d - _bmm(d, _bmm(o2, d))
    return d


def _causal_masks(lead=()):
    shape = lead + (CHUNK, CHUNK)
    r = lax.broadcasted_iota(jnp.int32, shape, len(lead))
    c = lax.broadcasted_iota(jnp.int32, shape, len(lead) + 1)
    return r >= c, r > c


def _decay_matrix(col, row, causal):
    return jnp.where(causal, jnp.exp(jnp.where(causal, col - row, 0.0)), 0.0)


def _gdn_kernel(qkv_ref, gate_ref, small_ref, convw_ref, nexpa_ref, dtb_ref, normw_ref,
                o_ref, pad_ref, state_ref, *, rows):
    t_first = pl.program_id(1) == 0

    @pl.when(t_first)
    def _():
        state_ref[...] = jnp.zeros(state_ref.shape, F32)

    y = _silu(_causal_conv(pad_ref, qkv_ref[...], convw_ref, 8, t_first))
    small = small_ref[...]
    g_all = nexpa_ref[...] * _softplus(small + dtb_ref[...])
    beta_all = jax.nn.sigmoid(small)
    causal, strict = _causal_masks((1,))
    nch = rows // CHUNK
    gcums = [_chunk_cumsum(g_all[ci * CHUNK:(ci + 1) * CHUNK, :]) for ci in range(nch)]
    gcums_t = [g.T for g in gcums]

    def stack(fn):
        return jnp.stack([fn(ci, h) for ci in range(nch) for h in range(GDN_HEADS)], axis=0)

    def head_cols(base):
        return stack(lambda ci, h: y[ci * CHUNK:(ci + 1) * CHUNK,
                                     base + h * GDN_HEAD_DIM:base + (h + 1) * GDN_HEAD_DIM])

    q, k, v = head_cols(0), head_cols(GDN_WIDTH), head_cols(2 * GDN_WIDTH)
    q = q * lax.rsqrt(jnp.sum(q * q, axis=-1, keepdims=True) + RMS_EPS) * (GDN_HEAD_DIM ** -0.5)
    k = k * lax.rsqrt(jnp.sum(k * k, axis=-1, keepdims=True) + RMS_EPS)
    beta = stack(lambda ci, h: beta_all[ci * CHUNK:(ci + 1) * CHUNK, SMALL_B + h:SMALL_B + h + 1])
    gc = stack(lambda ci, h: gcums[ci][:, SMALL_A + h:SMALL_A + h + 1])
    gr = stack(lambda ci, h: gcums_t[ci][SMALL_A + h:SMALL_A + h + 1, :])
    g_last = gc[:, CHUNK - 1:CHUNK, :]
    decay = _decay_matrix(gc, gr, causal)
    kb = k * beta
    t_mat = _unit_lower_inverse(jnp.where(strict, _bmm_nt(kb, k) * decay, 0.0))
    e_gc = jnp.exp(gc)
    u = _bmm(t_mat, v * beta)
    w = _bmm(t_mat, kb * e_gc)
    qk = _bmm_nt(q, k) * decay
    q_dec = q * e_gc
    k_dec = k * jnp.exp(g_last - gc)
    chunk_decay = jnp.exp(g_last)

    norm_w = normw_ref[...]
    state = state_ref[...]
    for ci in range(nch):
        sl = slice(ci * GDN_HEADS, (ci + 1) * GDN_HEADS)
        v_new = u[sl] - _bmm(w[sl], state)
        o = _bmm(q_dec[sl], state) + _bmm(qk[sl], v_new)
        state = state * chunk_decay[sl] + _bmm_tn(k_dec[sl], v_new)
        o = o * lax.rsqrt(jnp.mean(o * o, axis=-1, keepdims=True) + RMS_EPS) * norm_w
        for h in range(GDN_HEADS):
            c0 = h * GDN_HEAD_DIM
            gate = gate_ref[ci * CHUNK:(ci + 1) * CHUNK, c0:c0 + GDN_HEAD_DIM]
            o_ref[ci * CHUNK:(ci + 1) * CHUNK, c0:c0 + GDN_HEAD_DIM] = (o[h] * _silu(gate)).astype(o_ref.dtype)
    state_ref[...] = state


def _gated_deltanet(proj, conv_w, a_log, dt_bias, norm_w, *, bsz, seq, rows=256):
    nt = seq // rows
    nexpa = jnp.zeros((1, LANES), F32).at[0, SMALL_A:SMALL_A + GDN_HEADS].set(-jnp.exp(a_log))
    dtb = jnp.zeros((1, LANES), F32).at[0, SMALL_A:SMALL_A + GDN_HEADS].set(dt_bias)

    def rowblk(width, col):
        return pl.BlockSpec((rows, width), lambda b, t: (b * nt + t, col // width))

    def const(shape):
        return pl.BlockSpec(shape, lambda b, t: (0,) * len(shape))

    return pl.pallas_call(
        functools.partial(_gdn_kernel, rows=rows),
        out_shape=jax.ShapeDtypeStruct((bsz * seq, GDN_WIDTH), BF16),
        grid=(bsz, nt),
        in_specs=[rowblk(3 * GDN_WIDTH, COL_QKV), rowblk(GDN_WIDTH, COL_GATE), rowblk(LANES, COL_SMALL),
                  const((SHORT_CONV, 3 * GDN_WIDTH)), const((1, LANES)), const((1, LANES)),
                  const((1, GDN_HEAD_DIM))],
        out_specs=pl.BlockSpec((rows, GDN_WIDTH), lambda b, t: (b * nt + t, 0)),
        scratch_shapes=[pltpu.VMEM((8 + rows, 3 * GDN_WIDTH), F32),
                        pltpu.VMEM((GDN_HEADS, GDN_HEAD_DIM, GDN_HEAD_DIM), F32)],
        compiler_params=_params(("parallel", "arbitrary")), name="gated_deltanet",
    )(proj, proj, proj, conv_w, nexpa, dtb, norm_w.reshape(1, GDN_HEAD_DIM))


def _swa_kernel(sinks_ref, q_ref, kc_ref, kp_ref, vc_ref, vp_ref, cos_ref, sin_ref, cosp_ref, sinp_ref,
                o_ref):
    n = pl.program_id(1)
    blk = SWA_WINDOW
    cos, sin = cos_ref[...], sin_ref[...]
    cosp, sinp = cosp_ref[...], sinp_ref[...]

    def rope(x, c, s):
        return x * c + pltpu.roll(x, SWA_HEAD_DIM // 2, axis=1) * s

    rows = SWA_REP * blk
    qi = lax.broadcasted_iota(jnp.int32, (rows, 2 * blk), 0) & (blk - 1)
    kj = lax.broadcasted_iota(jnp.int32, (rows, 2 * blk), 1)
    dist = qi + blk - kj
    in_window = jnp.where(dist >= 0, jnp.where(dist < SWA_WINDOW, 1, 0), 0)
    has_prev = jnp.where(kj >= blk, 1, jnp.where(n > 0, 1, 0))
    mask = (in_window * has_prev) > 0
    head_of_row = lax.broadcasted_iota(jnp.int32, (rows, 1), 0) >> 7

    for g in range(SWA_KV_HEADS):
        d0 = g * SWA_HEAD_DIM
        q_rows = [rope(q_ref[:, (g * SWA_REP + j) * SWA_HEAD_DIM:(g * SWA_REP + j + 1) * SWA_HEAD_DIM], cos, sin)
                  for j in range(SWA_REP)]
        qs = jnp.concatenate(q_rows, axis=0)
        kband = jnp.concatenate([rope(kp_ref[:, d0:d0 + SWA_HEAD_DIM], cosp, sinp),
                                 rope(kc_ref[:, d0:d0 + SWA_HEAD_DIM], cos, sin)], axis=0)
        vband = jnp.concatenate([vp_ref[:, d0:d0 + SWA_HEAD_DIM], vc_ref[:, d0:d0 + SWA_HEAD_DIM]], axis=0)
        s = _bdot_nt(qs, kband) * (SWA_HEAD_DIM ** -0.5)
        s = jnp.where(mask, s, -jnp.inf)
        sink = jnp.zeros((rows, 1), F32)
        for j in range(SWA_REP):
            sink = jnp.where(head_of_row == j, sinks_ref[g * SWA_REP + j], sink)
        m = jnp.maximum(jnp.max(s, axis=-1, keepdims=True), sink)
        p = jnp.exp(s - m)
        denom = jnp.sum(p, axis=-1, keepdims=True) + jnp.exp(sink - m)
        o = _bdot(p / denom, vband)
        for j in range(SWA_REP):
            c0 = (g * SWA_REP + j) * SWA_HEAD_DIM
            o_ref[:, c0:c0 + SWA_HEAD_DIM] = o[j * blk:(j + 1) * blk, :].astype(o_ref.dtype)


def _sliding_window_attention(proj, sinks, cos_full, sin_signed, *, bsz, seq):
    blk = SWA_WINDOW
    nb = seq // blk
    kvw = SWA_KV_HEADS * SWA_HEAD_DIM

    def cur(width, col):
        return pl.BlockSpec((blk, width), lambda b, n: (b * nb + n, col // width))

    def prev(width, col):
        return pl.BlockSpec((blk, width), lambda b, n: (b * nb + jnp.maximum(n - 1, 0), col // width))

    tab_cur = pl.BlockSpec((blk, SWA_HEAD_DIM), lambda b, n: (n, 0))
    tab_prev = pl.BlockSpec((blk, SWA_HEAD_DIM), lambda b, n: (jnp.maximum(n - 1, 0), 0))
    return pl.pallas_call(
        _swa_kernel, out_shape=jax.ShapeDtypeStruct((bsz * seq, BRANCH_WIDTH), BF16),
        grid=(bsz, nb),
        in_specs=[pl.BlockSpec(memory_space=pltpu.SMEM),
                  cur(BRANCH_WIDTH, COL_SWA_Q), cur(kvw, COL_SWA_K), prev(kvw, COL_SWA_K),
                  cur(kvw, COL_SWA_V), prev(kvw, COL_SWA_V), tab_cur, tab_cur, tab_prev, tab_prev],
        out_specs=pl.BlockSpec((blk, BRANCH_WIDTH), lambda b, n: (b * nb + n, 0)),
        compiler_params=_params(("parallel", "parallel")), name="sliding_window_attention",
    )(sinks, proj, proj, proj, proj, proj, cos_full, sin_signed, cos_full, sin_signed)


def _conf_kernel(a_ref, gt_ref, convw_ref, convb_ref, lng_ref, lnb_ref, pww_ref, pwb_ref, wt_ref,
                 o_ref, wg_ref, pad_ref, shift_ref):
    wg_ref[...] = wt_ref[0].T.astype(BF16)
    t_first = pl.program_id(1) == 0
    h = a_ref[...] * jax.nn.sigmoid(gt_ref[...])
    h = _causal_conv(pad_ref, h, convw_ref, 32, t_first, shift_ref) + convb_ref[...]
    mu = jnp.mean(h, axis=-1, keepdims=True)
    c = h - mu
    var = jnp.mean(c * c, axis=-1, keepdims=True)
    h = _silu(c * lax.rsqrt(var + LN_EPS) * lng_ref[...] + lnb_ref[...])
    o_ref[...] = (jnp.dot(h.astype(BF16), pww_ref[...], preferred_element_type=F32)
                  + pwb_ref[...]).astype(o_ref.dtype)


def _conformer_conv(proj, conv_w, conv_b, ln_g, ln_b, pw_w, pw_b, wt, layer, *, bsz, seq, rows=256):
    nt = seq // rows
    w = CONF_WIDTH
    d = wt.shape[2]
    gate_cols = N_BRANCH * D_MODEL
    gate_rows = gate_cols // (bsz * nt)
    assert gate_rows * bsz * nt == gate_cols and gate_rows % LANES == 0
    gate0 = _REF_OFFS[11]

    def rowblk(col):
        return pl.BlockSpec((rows, w), lambda b, t: (b * nt + t, col // w))

    def const(shape):
        return pl.BlockSpec(shape, lambda b, t: (0,) * len(shape))

    wt_spec = pl.BlockSpec((pl.Element(1), pl.Element(gate_rows), pl.Element(d)),
                           lambda b, t: (layer, pl.multiple_of(gate0 + (b * nt + t) * gate_rows, 8), 0))
    return pl.pallas_call(
        _conf_kernel,
        out_shape=(jax.ShapeDtypeStruct((bsz * seq, w), BF16), jax.ShapeDtypeStruct((d, gate_cols), BF16)),
        grid=(bsz, nt),
        in_specs=[rowblk(COL_CONF), rowblk(COL_CONF + w), const((CONF_CONV_WIDTH, w)), const((1, w)),
                  const((1, w)), const((1, w)), const((w, w)), const((1, w)), wt_spec],
        out_specs=(pl.BlockSpec((rows, w), lambda b, t: (b * nt + t, 0)),
                   pl.BlockSpec((d, gate_rows), lambda b, t: (0, b * nt + t))),
        scratch_shapes=[pltpu.VMEM((32 + rows, w), F32),
                        pltpu.VMEM((rows + 8 * ((CONF_CONV_WIDTH - 1) // 8), w), F32)],
        compiler_params=_params(("parallel", "arbitrary")), name="conformer_conv",
    )(proj, proj, conv_w, conv_b.reshape(1, w), ln_g.reshape(1, w), ln_b.reshape(1, w),
      pw_w.astype(BF16), pw_b.reshape(1, w), wt)


def _ssd_kernel(z_ref, xs_ref, bc_ref, small_ref, wx_ref, wbc_ref, bx_ref, bbc_ref, nexpa_ref, dtb_ref,
                dskip_ref, normw_ref, o_ref, xpad_ref, bcpad_ref, state_ref, y_ref, *, rows):
    t_first = pl.program_id(1) == 0

    @pl.when(t_first)
    def _():
        state_ref[...] = jnp.zeros(state_ref.shape, F32)

    xs = _silu(_causal_conv(xpad_ref, xs_ref[...], wx_ref, 8, t_first) + bx_ref[...])
    bc = _silu(_causal_conv(bcpad_ref, bc_ref[...], wbc_ref, 8, t_first) + bbc_ref[...])
    dt_all = _softplus(small_ref[...] + dtb_ref[...])
    a_all = nexpa_ref[...] * dt_all
    gs = SSD_GROUPS * SSD_STATE
    nch = rows // CHUNK
    pairs = SSD_HEADS // 2
    pairs_per_group = pairs // SSD_GROUPS

    er = lax.broadcasted_iota(jnp.int32, (LANES, SSD_WIDTH), 0)
    ec = lax.broadcasted_iota(jnp.int32, (LANES, SSD_WIDTH), 1)
    expand = jnp.where(er == SMALL_DT + (ec >> 6), 1.0, 0.0).astype(F32)

    def per_head_cols(v):
        return jnp.dot(v, expand, preferred_element_type=F32, precision=lax.Precision.HIGHEST)

    xdt = xs * per_head_cols(dt_all)
    lane = lax.broadcasted_iota(jnp.int32, (1, LANES), 1)
    low = lane < SSD_HEAD_DIM
    row2 = lax.broadcasted_iota(jnp.int32, (2 * CHUNK, 1), 0)
    l_idx = lax.broadcasted_iota(jnp.int32, (CHUNK, LANES), 0)
    s_idx = lax.broadcasted_iota(jnp.int32, (CHUNK, LANES), 1) & (CHUNK - 1)
    causal2 = l_idx >= s_idx

    lhs, rhs, xdec, bms, cms, e_ac, h_dec = [], [], [], [], [], [], []
    for ci in range(nch):
        r0 = ci * CHUNK
        acs = _chunk_cumsum(a_all[r0:r0 + CHUNK, :])
        ac_cols = per_head_cols(acs)
        acs_t = jnp.concatenate([acs, acs], axis=0).T
        a_last_cols = ac_cols[CHUNK - 1:CHUNK, :]
        xdec_c = xdt[r0:r0 + CHUNK, :] * jnp.exp(a_last_cols - ac_cols)
        e_ac_c = jnp.exp(ac_cols)
        cb2 = []
        for g in range(SSD_GROUPS):
            bm = bc[r0:r0 + CHUNK, g * SSD_STATE:(g + 1) * SSD_STATE]
            cm = bc[r0:r0 + CHUNK, gs + g * SSD_STATE:gs + (g + 1) * SSD_STATE]
            cb2.append(_bdot_nt(cm, jnp.concatenate([bm, bm], axis=0)))
        for p in range(pairs):
            g = p // pairs_per_group
            c0 = p * LANES
            l0 = SMALL_DT + 2 * p
            a_row = jnp.where(low, acs_t[l0:l0 + 1, :], acs_t[l0 + 1:l0 + 2, :])
            seg = _decay_matrix(ac_cols[:, c0:c0 + LANES], a_row, causal2)
            lhs.append(cb2[g] * seg)
            xp = xdt[r0:r0 + CHUNK, c0:c0 + LANES]
            rhs.append(jnp.concatenate([jnp.where(low, xp, 0.0), jnp.where(low, 0.0, xp)], axis=0))
            xdec.append(xdec_c[:, c0:c0 + LANES])
            bms.append(bc[r0:r0 + CHUNK, g * SSD_STATE:(g + 1) * SSD_STATE])
            cms.append(bc[r0:r0 + CHUNK, gs + g * SSD_STATE:gs + (g + 1) * SSD_STATE])
            e_ac.append(e_ac_c[:, c0:c0 + LANES])
            a_last = jnp.where(row2 < SSD_HEAD_DIM, acs[CHUNK - 1:CHUNK, l0:l0 + 1],
                               acs[CHUNK - 1:CHUNK, l0 + 1:l0 + 2])
            h_dec.append(jnp.exp(a_last))

    y_diag = _bmm(jnp.stack(lhs), jnp.stack(rhs))
    states = _bmm_tn(jnp.stack(xdec), jnp.stack(bms))
    cms, e_ac, h_dec = jnp.stack(cms), jnp.stack(e_ac), jnp.stack(h_dec)

    h = state_ref[...]
    for ci in range(nch):
        sl = slice(ci * pairs, (ci + 1) * pairs)
        y = y_diag[sl] + _bmm_nt(cms[sl], h) * e_ac[sl]
        h = h * h_dec[sl] + states[sl]
        for p in range(pairs):
            y_ref[ci * CHUNK:(ci + 1) * CHUNK, p * LANES:(p + 1) * LANES] = y[p]
    state_ref[...] = h

    y = y_ref[...] + dskip_ref[...] * xs
    y = y * _silu(z_ref[...])
    y = y * lax.rsqrt(jnp.mean(y * y, axis=-1, keepdims=True) + RMS_EPS) * normw_ref[...]
    o_ref[...] = y.astype(o_ref.dtype)


def _mamba2_ssd(proj, conv_w, conv_b, a_log, dt_bias, d_skip, norm_w, *, bsz, seq, rows=256):
    nt = seq // rows
    w = SSD_WIDTH
    nexpa = jnp.zeros((1, LANES), F32).at[0, SMALL_DT:SMALL_DT + SSD_HEADS].set(-jnp.exp(a_log))
    dtb = jnp.zeros((1, LANES), F32).at[0, SMALL_DT:SMALL_DT + SSD_HEADS].set(dt_bias)
    d_full = jnp.repeat(d_skip, SSD_HEAD_DIM).reshape(1, w)

    def rowblk(width, col):
        return pl.BlockSpec((rows, width), lambda b, t: (b * nt + t, col // width))

    def const(shape):
        return pl.BlockSpec(shape, lambda b, t: (0,) * len(shape))

    return pl.pallas_call(
        functools.partial(_ssd_kernel, rows=rows),
        out_shape=jax.ShapeDtypeStruct((bsz * seq, w), BF16),
        grid=(bsz, nt),
        in_specs=[rowblk(w, COL_Z), rowblk(w, COL_XS), rowblk(SSD_BC_WIDTH, COL_BC), rowblk(LANES, COL_SMALL),
                  const((SHORT_CONV, w)), const((SHORT_CONV, SSD_BC_WIDTH)), const((1, w)),
                  const((1, SSD_BC_WIDTH)), const((1, LANES)), const((1, LANES)), const((1, w)), const((1, w))],
        out_specs=pl.BlockSpec((rows, w), lambda b, t: (b * nt + t, 0)),
        scratch_shapes=[pltpu.VMEM((8 + rows, w), F32), pltpu.VMEM((8 + rows, SSD_BC_WIDTH), F32),
                        pltpu.VMEM((SSD_HEADS // 2, 2 * SSD_HEAD_DIM, SSD_STATE), F32),
                        pltpu.VMEM((rows, w), F32)],
        compiler_params=_params(("parallel", "arbitrary")), name="mamba2_ssd",
    )(proj, proj, proj, proj, conv_w[:, :w], conv_w[:, w:], conv_b[:w].reshape(1, w),
      conv_b[w:].reshape(1, SSD_BC_WIDTH), nexpa, dtb, d_full, norm_w.reshape(1, w))


PREP_TN = 512


def _regroup_kernel(wt_ref, *rest, small_at):
    o_ref = rest[-1]
    j = pl.program_id(1)

    @pl.when(j != small_at)
    def _():
        o_ref[...] = wt_ref[0].T.astype(BF16)

    if len(rest) == 2:
        @pl.when(j == small_at)
        def _():
            used = SMALL_DT + SSD_HEADS
            pad = jnp.zeros((PREP_TN - used, wt_ref.shape[2]), F32)
            rows = jnp.concatenate([wt_ref[0, 0:SMALL_DT, :], rest[0][0], pad], axis=0)
            o_ref[...] = rows.T.astype(BF16)


def _regroup_weights(wt, n_blocks, row_start, *, small_at=-1, dt_start=None, name):
    layers, _, d = wt.shape
    in_specs = [pl.BlockSpec((pl.Element(1), pl.Element(PREP_TN), pl.Element(d)),
                             lambda l, j: (l, pl.multiple_of(row_start(j), 8), 0))]
    args = [wt]
    if dt_start is not None:
        in_specs.append(pl.BlockSpec((pl.Element(1), pl.Element(SSD_HEADS), pl.Element(d)),
                                     lambda l, j: (l, dt_start, 0)))
        args.append(wt)
    return pl.pallas_call(
        functools.partial(_regroup_kernel, small_at=small_at),
        out_shape=jax.ShapeDtypeStruct((layers, d, n_blocks * PREP_TN), BF16),
        grid=(layers, n_blocks), in_specs=in_specs,
        out_specs=pl.BlockSpec((None, d, PREP_TN), lambda l, j: (l, 0, j)),
        compiler_params=_params(("parallel", "parallel")), name=name)(*args)


def _rearranged_in_weights(w_in):
    o = _REF_OFFS
    wt = jnp.swapaxes(w_in, 1, 2)
    q0, conf0, kv, small = (c // PREP_TN for c in (COL_SWA_Q, COL_CONF, COL_SWA_K, COL_SMALL))
    assert o[2] == COL_SWA_Q and o[3] - o[2] == GDN_HEADS == SMALL_B and o[4] - o[2] == SMALL_DT

    def mixer_rows(j):
        c = j * PREP_TN
        return jnp.where(j < q0, c, jnp.where(j < conf0, c + (o[4] - COL_SWA_Q),
                         jnp.where(j < kv, c + (o[7] - COL_CONF), jnp.where(j < small, o[5], o[2]))))

    w_small = _regroup_weights(wt, PROJ_COLS // PREP_TN, mixer_rows, small_at=small, dt_start=o[10],
                               name="regroup_mixer_weights")
    return w_small, wt


def _rotary_tables(seq):
    inv_freq = ROPE_THETA ** (-jnp.arange(0, SWA_HEAD_DIM, 2, dtype=F32) / SWA_HEAD_DIM)
    ang = jnp.arange(seq, dtype=F32)[:, None] * inv_freq[None, :]
    cos, sin = jnp.cos(ang), jnp.sin(ang)
    return jnp.concatenate([cos, cos], axis=1), jnp.concatenate([-sin, sin], axis=1)


def kernel(x, w_in, gdn_conv_w, gdn_a_log, gdn_dt_bias, gdn_norm_w, swa_sinks, conf_conv_w, conf_conv_b, conf_ln_g, conf_ln_b, conf_pw_w, conf_pw_b, ssd_conv_w, ssd_conv_b, ssd_a_log, ssd_dt_bias, ssd_d, ssd_norm_w, w_branch, w_out, ln1_g, ln1_b, w_gate_up, w_down, ln2_g, ln2_b):
    bsz, seq, d = x.shape
    m = bsz * seq
    cos_full, sin_signed = _rotary_tables(seq)
    xf = x.reshape(m, d)
    xb = xf.astype(BF16)
    w_small, w_in_t = _rearranged_in_weights(w_in)
    for i in range(DEPTH):
        proj = _matmul(xb, w_small, i, tm=512, tn=1536, out_dtype=F32, name="in_proj")
        o_a = _gated_deltanet(proj, gdn_conv_w[i], gdn_a_log[i], gdn_dt_bias[i], gdn_norm_w[i],
                              bsz=bsz, seq=seq)
        o_b = _sliding_window_attention(proj, swa_sinks[i], cos_full, sin_signed, bsz=bsz, seq=seq)
        o_c, w_gates = _conformer_conv(proj, conf_conv_w[i], conf_conv_b[i], conf_ln_g[i], conf_ln_b[i],
                                       conf_pw_w[i], conf_pw_b[i], w_in_t, i, bsz=bsz, seq=seq)
        o_d = _mamba2_ssd(proj, ssd_conv_w[i], ssd_conv_b[i], ssd_a_log[i], ssd_dt_bias[i], ssd_d[i],
                          ssd_norm_w[i], bsz=bsz, seq=seq)
        merged = _merge(xb, (o_a, o_b, o_c, o_d), w_gates, w_branch, i, tm=512, tn=256)
        s1 = _matmul(merged, w_out, i, tm=1024, tn=512, out_dtype=F32, resid=xf, name="out_proj")
        xf, xb = _layer_norm(s1, ln1_g[i], ln1_b[i], tr=256)
        act, w_down_b = _swiglu(xb, w_gate_up, w_down, i, tm=1024, tn=256)
        s2 = _matmul(act, w_down_b[None], 0, tm=256, tn=512, out_dtype=F32, resid=xf, name="down_proj")
        xf, xb = _layer_norm(s2, ln2_g[i], ln2_b[i], tr=256)
    return xf.reshape(bsz, seq, d)
```

```python
import functools

import jax
import jax.numpy as jnp
import numpy as np
from jax import lax
from jax.experimental import pallas as pl
from jax.experimental.pallas import tpu as pltpu

F32 = jnp.float32
BF16 = jnp.bfloat16

D_MODEL = 4096
DEPTH = 2
N_BRANCH = 4
BRANCH_WIDTH = D_MODEL // 4

GDN_HEAD_DIM = 128
GDN_HEADS = BRANCH_WIDTH // GDN_HEAD_DIM
GDN_WIDTH = GDN_HEADS * GDN_HEAD_DIM
CHUNK = 64
SHORT_CONV = 4

SWA_HEAD_DIM = 128
SWA_Q_HEADS = BRANCH_WIDTH // SWA_HEAD_DIM
SWA_KV_HEADS = SWA_Q_HEADS // 4
SWA_REP = SWA_Q_HEADS // SWA_KV_HEADS
SWA_WINDOW = 128
ROPE_THETA = 10000.0

CONF_WIDTH = BRANCH_WIDTH
CONF_CONV_WIDTH = 31

SSD_HEAD_DIM = 64
SSD_WIDTH = BRANCH_WIDTH
SSD_HEADS = SSD_WIDTH // SSD_HEAD_DIM
SSD_GROUPS = 2
SSD_HEADS_PER_GROUP = SSD_HEADS // SSD_GROUPS
SSD_STATE = 128
SSD_BC_WIDTH = 2 * SSD_GROUPS * SSD_STATE

FFN_HIDDEN = -(-8 * D_MODEL // (3 * 256)) * 256

ALPHA = (2.0 * DEPTH) ** 0.25
LN_EPS = 1e-5
RMS_EPS = 1e-6

LANES = 128
VMEM_LIMIT = 52 * 1024 * 1024

COL_QKV = 0
COL_GATE = COL_QKV + 3 * GDN_WIDTH
COL_SWA_Q = COL_GATE + GDN_WIDTH
COL_CONF = COL_SWA_Q + BRANCH_WIDTH
COL_Z = COL_CONF + 2 * CONF_WIDTH
COL_XS = COL_Z + SSD_WIDTH
COL_BC = COL_XS + SSD_WIDTH
COL_SWA_K = COL_BC + SSD_BC_WIDTH
COL_SWA_V = COL_SWA_K + SWA_KV_HEADS * SWA_HEAD_DIM
COL_SMALL = COL_SWA_V + SWA_KV_HEADS * SWA_HEAD_DIM
SMALL_A = 0
SMALL_B = GDN_HEADS
SMALL_DT = 2 * GDN_HEADS
PROJ_COLS = 10752

_REF_SPLITS = (3 * GDN_WIDTH, GDN_WIDTH, GDN_HEADS, GDN_HEADS, BRANCH_WIDTH, 256, 256,
               2 * CONF_WIDTH, SSD_WIDTH, SSD_WIDTH + SSD_BC_WIDTH, SSD_HEADS, N_BRANCH * D_MODEL)
_REF_OFFS = np.concatenate([[0], np.cumsum(_REF_SPLITS)]).tolist()


def _params(sem, vmem=VMEM_LIMIT):
    return pltpu.CompilerParams(dimension_semantics=sem, vmem_limit_bytes=vmem)


def _bdot(a, b):
    return jnp.dot(a.astype(BF16), b.astype(BF16), preferred_element_type=F32)


def _bdot_nt(a, b):
    return lax.dot_general(a.astype(BF16), b.astype(BF16), (((1,), (1,)), ((), ())),
                           preferred_element_type=F32)


def _bdot_tn(a, b):
    return lax.dot_general(a.astype(BF16), b.astype(BF16), (((0,), (0,)), ((), ())),
                           preferred_element_type=F32)


def _silu(x):
    return x * jax.nn.sigmoid(x)


def _softplus(x):
    return jnp.maximum(x, 0.0) + jnp.log(1.0 + jnp.exp(-jnp.abs(x)))


def _cast_weights_once(w_refs, wb_refs):
    @pl.when(pl.program_id(1) == 0)
    def _():
        for w_ref, wb_ref in zip(w_refs, wb_refs):
            wb_ref[...] = w_ref[...].astype(BF16)


def _mm_kernel(x_ref, w_ref, *rest, has_resid, cast_w):
    rest = list(rest)
    r_ref = rest.pop(0) if has_resid else None
    o_ref = rest.pop(0)
    if cast_w:
        wb_ref = rest.pop(0)
        _cast_weights_once([w_ref], [wb_ref])
        w_ref = wb_ref
    acc = jnp.dot(x_ref[...], w_ref[...], preferred_element_type=F32)
    if has_resid:
        acc = ALPHA * r_ref[...] + acc
    o_ref[...] = acc.astype(o_ref.dtype)


def _matmul(x, w, layer, *, tm, tn, out_dtype, resid=None, name):
    m, k = x.shape
    n = w.shape[2]
    cast_w = w.dtype != BF16
    in_specs = [pl.BlockSpec((tm, k), lambda j, i: (i, 0)),
                pl.BlockSpec((None, k, tn), lambda j, i: (layer, 0, j))]
    args = [x, w]
    if resid is not None:
        in_specs.append(pl.BlockSpec((tm, tn), lambda j, i: (i, j)))
        args.append(resid)
    return pl.pallas_call(
        functools.partial(_mm_kernel, has_resid=resid is not None, cast_w=cast_w),
        out_shape=jax.ShapeDtypeStruct((m, n), out_dtype),
        grid=(n // tn, m // tm), in_specs=in_specs,
        out_specs=pl.BlockSpec((tm, tn), lambda j, i: (i, j)),
        scratch_shapes=[pltpu.VMEM((k, tn), BF16)] if cast_w else [],
        compiler_params=_params(("parallel", "arbitrary" if cast_w else "parallel")), name=name)(*args)


def _swiglu_kernel(x_ref, wg0_ref, wg1_ref, wu0_ref, wu1_ref, wd_ref, o_ref, wdb_ref, wgb_ref, wub_ref):
    half = wg0_ref.shape[0]
    _cast_weights_once([wg0_ref, wg1_ref, wu0_ref, wu1_ref],
                       [wgb_ref.at[0:half], wgb_ref.at[half:2 * half], wub_ref.at[0:half], wub_ref.at[half:2 * half]])
    x = x_ref[...]
    g = jnp.dot(x, wgb_ref[...], preferred_element_type=F32)
    u = jnp.dot(x, wub_ref[...], preferred_element_type=F32)
    o_ref[...] = (_silu(g) * u).astype(o_ref.dtype)
    wdb_ref[...] = wd_ref[...].astype(BF16)


def _swiglu(x, w_gate_up, w_down, layer, *, tm, tn):
    m, k = x.shape
    hidden = w_gate_up.shape[2] // 2
    nb = hidden // tn
    n_inner = m // tm
    d_out = w_down.shape[2]
    wd_rows = hidden // (nb * n_inner)
    assert wd_rows * nb * n_inner == hidden and wd_rows % 16 == 0

    def weight_half(which, up, order):
        def index(j, i):
            ahead = jnp.minimum(j + jnp.where(i > order, 1, 0), nb - 1)
            return (layer, which, ahead + (nb if up else 0))
        return pl.BlockSpec((None, k // 2, tn), index)

    return pl.pallas_call(
        _swiglu_kernel,
        out_shape=(jax.ShapeDtypeStruct((m, hidden), BF16), jax.ShapeDtypeStruct((hidden, d_out), BF16)),
        grid=(nb, n_inner),
        in_specs=[pl.BlockSpec((tm, k), lambda j, i: (i, 0)),
                  weight_half(0, False, 0), weight_half(1, False, 1),
                  weight_half(0, True, 2), weight_half(1, True, 3),
                  pl.BlockSpec((None, wd_rows, d_out), lambda j, i: (layer, j * n_inner + i, 0))],
        out_specs=(pl.BlockSpec((tm, tn), lambda j, i: (i, j)),
                   pl.BlockSpec((wd_rows, d_out), lambda j, i: (j * n_inner + i, 0))),
        scratch_shapes=[pltpu.VMEM((k, tn), BF16), pltpu.VMEM((k, tn), BF16)],
        compiler_params=_params(("arbitrary", "arbitrary")), name="swiglu",
    )(x, *([w_gate_up] * 4), w_down)


def _merge_kernel(x_ref, oa_ref, ob_ref, oc_ref, od_ref,
                  wg0, wg1, wg2, wg3, wb0, wb1, wb2, wb3, wo_ref, o_ref, wob_ref, wbb0, wbb1, wbb2, wbb3):
    _cast_weights_once([wb0, wb1, wb2, wb3], [wbb0, wbb1, wbb2, wbb3])
    wob_ref[...] = wo_ref[...].astype(BF16)
    x = x_ref[...]
    acc = None
    for o_r, wg, wb in ((oa_ref, wg0, wbb0), (ob_ref, wg1, wbb1), (oc_ref, wg2, wbb2), (od_ref, wg3, wbb3)):
        gate = jax.nn.sigmoid(jnp.dot(x, wg[...], preferred_element_type=F32))
        up = jnp.dot(o_r[...], wb[...], preferred_element_type=F32)
        acc = gate * up if acc is None else acc + gate * up
    o_ref[...] = acc.astype(o_ref.dtype)


def _merge(x, branches, w_gates, w_branch, w_out, layer, *, tm, tn):
    m, k = x.shape
    d = w_branch.shape[3]
    nb = d // tn
    bw = w_branch.shape[2]
    n_inner = m // tm
    wo_rows = w_out.shape[1] // (nb * n_inner)
    assert wo_rows * nb * n_inner == w_out.shape[1] and wo_rows % 16 == 0
    in_specs = [pl.BlockSpec((tm, k), lambda j, i: (i, 0))]
    in_specs += [pl.BlockSpec((tm, bw), lambda j, i: (i, 0)) for _ in range(N_BRANCH)]
    in_specs += [pl.BlockSpec((k, tn), functools.partial(lambda j, i, g: (0, g * nb + j), g=g))
                 for g in range(N_BRANCH)]
    in_specs += [pl.BlockSpec((None, None, bw, tn), functools.partial(lambda j, i, g: (layer, g, 0, j), g=g))
                 for g in range(N_BRANCH)]
    in_specs.append(pl.BlockSpec((None, wo_rows, w_out.shape[2]), lambda j, i: (layer, j * n_inner + i, 0)))
    return pl.pallas_call(
        _merge_kernel,
        out_shape=(jax.ShapeDtypeStruct((m, d), BF16), jax.ShapeDtypeStruct(w_out.shape[1:], BF16)),
        grid=(nb, n_inner), in_specs=in_specs,
        out_specs=(pl.BlockSpec((tm, tn), lambda j, i: (i, j)),
                   pl.BlockSpec((wo_rows, w_out.shape[2]), lambda j, i: (j * n_inner + i, 0))),
        scratch_shapes=[pltpu.VMEM((bw, tn), BF16) for _ in range(N_BRANCH)],
        compiler_params=_params(("parallel", "arbitrary")), name="merge",
    )(x, *branches, *([w_gates] * N_BRANCH), *([w_branch] * N_BRANCH), w_out)


def _ln_kernel(s_ref, g_ref, b_ref, of_ref, ob_ref):
    s = s_ref[...]
    mu = jnp.mean(s, axis=-1, keepdims=True)
    c = s - mu
    var = jnp.mean(c * c, axis=-1, keepdims=True)
    y = c * lax.rsqrt(var + LN_EPS) * g_ref[...] + b_ref[...]
    of_ref[...] = y
    ob_ref[...] = y.astype(BF16)


def _layer_norm(s, g, b, *, tr):
    m, d = s.shape
    row = pl.BlockSpec((tr, d), lambda i: (i, 0))
    vec = pl.BlockSpec((1, d), lambda i: (0, 0))
    return pl.pallas_call(
        _ln_kernel,
        out_shape=(jax.ShapeDtypeStruct((m, d), F32), jax.ShapeDtypeStruct((m, d), BF16)),
        grid=(m // tr,), in_specs=[row, vec, vec], out_specs=(row, row),
        compiler_params=_params(("parallel",)), name="layer_norm")(s, g.reshape(1, d), b.reshape(1, d))


def _causal_conv(pad_ref, x, w_ref, halo, t_first, shift_ref=None):
    rows = x.shape[0]
    k = w_ref.shape[0]

    @pl.when(t_first)
    def _():
        pad_ref[0:halo, :] = jnp.zeros((halo, x.shape[1]), F32)

    pad_ref[halo:halo + rows, :] = x
    acc = None
    sublanes = 8
    first = halo - (k - 1)
    for phase in range(min(sublanes, k)):
        taps = list(range(phase, k, sublanes))
        lo = first + taps[0]
        if shift_ref is not None:
            span = rows + taps[-1] - taps[0]
            shift_ref[0:span, :] = pad_ref[lo:lo + span, :]
        for j in taps:
            if shift_ref is not None:
                window = shift_ref[j - taps[0]:j - taps[0] + rows, :]
            else:
                window = pad_ref[first + j:first + j + rows, :]
            term = w_ref[j:j + 1, :] * window
            acc = term if acc is None else acc + term
    pad_ref[0:halo, :] = x[rows - halo:rows, :]
    return acc


def _chunk_cumsum(x):
    r = lax.broadcasted_iota(jnp.int32, (CHUNK, CHUNK), 0)
    c = lax.broadcasted_iota(jnp.int32, (CHUNK, CHUNK), 1)
    tril = jnp.where(r >= c, 1.0, 0.0).astype(F32)
    return jnp.dot(tril, x, preferred_element_type=F32, precision=lax.Precision.HIGHEST)


def _bmm(a, b):
    return lax.dot_general(a.astype(BF16), b.astype(BF16), (((2,), (1,)), ((0,), (0,))),
                           preferred_element_type=F32)


def _bmm_nt(a, b):
    return lax.dot_general(a.astype(BF16), b.astype(BF16), (((2,), (2,)), ((0,), (0,))),
                           preferred_element_type=F32)


def _bmm_tn(a, b):
    return lax.dot_general(a.astype(BF16), b.astype(BF16), (((1,), (1,)), ((0,), (0,))),
                           preferred_element_type=F32)


def _unit_lower_inverse(a):
    r = lax.broadcasted_iota(jnp.int32, (1, CHUNK, CHUNK), 1)
    c = lax.broadcasted_iota(jnp.int32, (1, CHUNK, CHUNK), 2)
    same16 = (r >> 4) == (c >> 4)
    same32 = (r >> 5) == (c >> 5)
    eye = jnp.where(r == c, 1.0, 0.0).astype(F32)
    ad = jnp.where(same16, a, 0.0)
    o1 = jnp.where(same32, a, 0.0) - ad
    o2 = jnp.where(same32, 0.0, a)
    a2 = _bmm(ad, ad)
    d = eye - ad
    a4 = _bmm(a2, a2)
    d = d + _bmm(d, a2)
    a8 = _bmm(a4, a4)
    d = d + _bmm(d, a4)
    d = d + _bmm(d, a8)
    d = d - _bmm(d, _bmm(o1, d))
    d = d - _bmm(d, _bmm(o2, d))
    return d


def _causal_masks(lead=()):
    shape = lead + (CHUNK, CHUNK)
    r = lax.broadcasted_iota(jnp.int32, shape, len(lead))
    c = lax.broadcasted_iota(jnp.int32, shape, len(lead) + 1)
    return r >= c, r > c


def _decay_matrix(col, row, causal):
    return jnp.where(causal, jnp.exp(jnp.where(causal, col - row, 0.0)), 0.0)


def _gdn_kernel(qkv_ref, gate_ref, small_ref, convw_ref, nexpa_ref, dtb_ref, normw_ref,
                o_ref, pad_ref, state_ref, *, rows):
    t_first = pl.program_id(1) == 0

    @pl.when(t_first)
    def _():
        state_ref[...] = jnp.zeros(state_ref.shape, F32)

    y = _silu(_causal_conv(pad_ref, qkv_ref[...], convw_ref, 8, t_first))
    small = small_ref[...]
    g_all = nexpa_ref[...] * _softplus(small + dtb_ref[...])
    beta_all = jax.nn.sigmoid(small)
    causal, strict = _causal_masks((1,))
    nch = rows // CHUNK
    gcums = [_chunk_cumsum(g_all[ci * CHUNK:(ci + 1) * CHUNK, :]) for ci in range(nch)]
    gcums_t = [g.T for g in gcums]

    def stack(fn):
        return jnp.stack([fn(ci, h) for ci in range(nch) for h in range(GDN_HEADS)], axis=0)

    def head_cols(base):
        return stack(lambda ci, h: y[ci * CHUNK:(ci + 1) * CHUNK,
                                     base + h * GDN_HEAD_DIM:base + (h + 1) * GDN_HEAD_DIM])

    q, k, v = head_cols(0), head_cols(GDN_WIDTH), head_cols(2 * GDN_WIDTH)
    q = q * lax.rsqrt(jnp.sum(q * q, axis=-1, keepdims=True) + RMS_EPS) * (GDN_HEAD_DIM ** -0.5)
    k = k * lax.rsqrt(jnp.sum(k * k, axis=-1, keepdims=True) + RMS_EPS)
    beta = stack(lambda ci, h: beta_all[ci * CHUNK:(ci + 1) * CHUNK, SMALL_B + h:SMALL_B + h + 1])
    gc = stack(lambda ci, h: gcums[ci][:, SMALL_A + h:SMALL_A + h + 1])
    gr = stack(lambda ci, h: gcums_t[ci][SMALL_A + h:SMALL_A + h + 1, :])
    g_last = gc[:, CHUNK - 1:CHUNK, :]
    decay = _decay_matrix(gc, gr, causal)
    kb = k * beta
    t_mat = _unit_lower_inverse(jnp.where(strict, _bmm_nt(kb, k) * decay, 0.0))
    e_gc = jnp.exp(gc)
    u = _bmm(t_mat, v * beta)
    w = _bmm(t_mat, kb * e_gc)
    qk = _bmm_nt(q, k) * decay
    q_dec = q * e_gc
    k_dec = k * jnp.exp(g_last - gc)
    chunk_decay = jnp.exp(g_last)

    norm_w = normw_ref[...]
    state = state_ref[...]
    for ci in range(nch):
        sl = slice(ci * GDN_HEADS, (ci + 1) * GDN_HEADS)
        v_new = u[sl] - _bmm(w[sl], state)
        o = _bmm(q_dec[sl], state) + _bmm(qk[sl], v_new)
        state = state * chunk_decay[sl] + _bmm_tn(k_dec[sl], v_new)
        o = o * lax.rsqrt(jnp.mean(o * o, axis=-1, keepdims=True) + RMS_EPS) * norm_w
        for h in range(GDN_HEADS):
            c0 = h * GDN_HEAD_DIM
            gate = gate_ref[ci * CHUNK:(ci + 1) * CHUNK, c0:c0 + GDN_HEAD_DIM]
            o_ref[ci * CHUNK:(ci + 1) * CHUNK, c0:c0 + GDN_HEAD_DIM] = (o[h] * _silu(gate)).astype(o_ref.dtype)
    state_ref[...] = state


def _gated_deltanet(proj, conv_w, a_log, dt_bias, norm_w, *, bsz, seq, rows=256):
    nt = seq // rows
    nexpa = jnp.zeros((1, LANES), F32).at[0, SMALL_A:SMALL_A + GDN_HEADS].set(-jnp.exp(a_log))
    dtb = jnp.zeros((1, LANES), F32).at[0, SMALL_A:SMALL_A + GDN_HEADS].set(dt_bias)

    def rowblk(width, col):
        return pl.BlockSpec((rows, width), lambda b, t: (b * nt + t, col // width))

    def const(shape):
        return pl.BlockSpec(shape, lambda b, t: (0,) * len(shape))

    return pl.pallas_call(
        functools.partial(_gdn_kernel, rows=rows),
        out_shape=jax.ShapeDtypeStruct((bsz * seq, GDN_WIDTH), BF16),
        grid=(bsz, nt),
        in_specs=[rowblk(3 * GDN_WIDTH, COL_QKV), rowblk(GDN_WIDTH, COL_GATE), rowblk(LANES, COL_SMALL),
                  const((SHORT_CONV, 3 * GDN_WIDTH)), const((1, LANES)), const((1, LANES)),
                  const((1, GDN_HEAD_DIM))],
        out_specs=pl.BlockSpec((rows, GDN_WIDTH), lambda b, t: (b * nt + t, 0)),
        scratch_shapes=[pltpu.VMEM((8 + rows, 3 * GDN_WIDTH), F32),
                        pltpu.VMEM((GDN_HEADS, GDN_HEAD_DIM, GDN_HEAD_DIM), F32)],
        compiler_params=_params(("parallel", "arbitrary")), name="gated_deltanet",
    )(proj, proj, proj, conv_w, nexpa, dtb, norm_w.reshape(1, GDN_HEAD_DIM))


def _swa_kernel(sinks_ref, q_ref, kc_ref, kp_ref, vc_ref, vp_ref, cos_ref, sin_ref, cosp_ref, sinp_ref,
                o_ref):
    n = pl.program_id(1)
    blk = SWA_WINDOW
    cos, sin = cos_ref[...], sin_ref[...]
    cosp, sinp = cosp_ref[...], sinp_ref[...]

    def rope(x, c, s):
        return x * c + pltpu.roll(x, SWA_HEAD_DIM // 2, axis=1) * s

    rows = SWA_REP * blk
    qi = lax.broadcasted_iota(jnp.int32, (rows, 2 * blk), 0) & (blk - 1)
    kj = lax.broadcasted_iota(jnp.int32, (rows, 2 * blk), 1)
    dist = qi + blk - kj
    in_window = jnp.where(dist >= 0, jnp.where(dist < SWA_WINDOW, 1, 0), 0)
    has_prev = jnp.where(kj >= blk, 1, jnp.where(n > 0, 1, 0))
    mask = (in_window * has_prev) > 0
    head_of_row = lax.broadcasted_iota(jnp.int32, (rows, 1), 0) >> 7

    for g in range(SWA_KV_HEADS):
        d0 = g * SWA_HEAD_DIM
        q_rows = [rope(q_ref[:, (g * SWA_REP + j) * SWA_HEAD_DIM:(g * SWA_REP + j + 1) * SWA_HEAD_DIM], cos, sin)
                  for j in range(SWA_REP)]
        qs = jnp.concatenate(q_rows, axis=0)
        kband = jnp.concatenate([rope(kp_ref[:, d0:d0 + SWA_HEAD_DIM], cosp, sinp),
                                 rope(kc_ref[:, d0:d0 + SWA_HEAD_DIM], cos, sin)], axis=0)
        vband = jnp.concatenate([vp_ref[:, d0:d0 + SWA_HEAD_DIM], vc_ref[:, d0:d0 + SWA_HEAD_DIM]], axis=0)
        s = _bdot_nt(qs, kband) * (SWA_HEAD_DIM ** -0.5)
        s = jnp.where(mask, s, -jnp.inf)
        sink = jnp.zeros((rows, 1), F32)
        for j in range(SWA_REP):
            sink = jnp.where(head_of_row == j, sinks_ref[g * SWA_REP + j], sink)
        m = jnp.maximum(jnp.max(s, axis=-1, keepdims=True), sink)
        p = jnp.exp(s - m)
        denom = jnp.sum(p, axis=-1, keepdims=True) + jnp.exp(sink - m)
        o = _bdot(p / denom, vband)
        for j in range(SWA_REP):
            c0 = (g * SWA_REP + j) * SWA_HEAD_DIM
            o_ref[:, c0:c0 + SWA_HEAD_DIM] = o[j * blk:(j + 1) * blk, :].astype(o_ref.dtype)


def _sliding_window_attention(proj, sinks, cos_full, sin_signed, *, bsz, seq):
    blk = SWA_WINDOW
    nb = seq // blk
    kvw = SWA_KV_HEADS * SWA_HEAD_DIM

    def cur(width, col):
        return pl.BlockSpec((blk, width), lambda b, n: (b * nb + n, col // width))

    def prev(width, col):
        return pl.BlockSpec((blk, width), lambda b, n: (b * nb + jnp.maximum(n - 1, 0), col // width))

    tab_cur = pl.BlockSpec((blk, SWA_HEAD_DIM), lambda b, n: (n, 0))
    tab_prev = pl.BlockSpec((blk, SWA_HEAD_DIM), lambda b, n: (jnp.maximum(n - 1, 0), 0))
    return pl.pallas_call(
        _swa_kernel, out_shape=jax.ShapeDtypeStruct((bsz * seq, BRANCH_WIDTH), BF16),
        grid=(bsz, nb),
        in_specs=[pl.BlockSpec(memory_space=pltpu.SMEM),
                  cur(BRANCH_WIDTH, COL_SWA_Q), cur(kvw, COL_SWA_K), prev(kvw, COL_SWA_K),
                  cur(kvw, COL_SWA_V), prev(kvw, COL_SWA_V), tab_cur, tab_cur, tab_prev, tab_prev],
        out_specs=pl.BlockSpec((blk, BRANCH_WIDTH), lambda b, n: (b * nb + n, 0)),
        compiler_params=_params(("parallel", "parallel")), name="sliding_window_attention",
    )(sinks, proj, proj, proj, proj, proj, cos_full, sin_signed, cos_full, sin_signed)


def _conf_kernel(a_ref, gt_ref, convw_ref, convb_ref, lng_ref, lnb_ref, pww_ref, pwb_ref, wt_ref,
                 o_ref, wg_ref, pad_ref, shift_ref):
    wg_ref[...] = wt_ref[0].T.astype(BF16)
    t_first = pl.program_id(1) == 0
    h = a_ref[...] * jax.nn.sigmoid(gt_ref[...])
    h = _causal_conv(pad_ref, h, convw_ref, 32, t_first, shift_ref) + convb_ref[...]
    mu = jnp.mean(h, axis=-1, keepdims=True)
    c = h - mu
    var = jnp.mean(c * c, axis=-1, keepdims=True)
    h = _silu(c * lax.rsqrt(var + LN_EPS) * lng_ref[...] + lnb_ref[...])
    o_ref[...] = (jnp.dot(h.astype(BF16), pww_ref[...], preferred_element_type=F32)
                  + pwb_ref[...]).astype(o_ref.dtype)


def _conformer_conv(proj, conv_w, conv_b, ln_g, ln_b, pw_w, pw_b, wt, layer, *, bsz, seq, rows=256):
    nt = seq // rows
    w = CONF_WIDTH
    d = wt.shape[2]
    gate_cols = N_BRANCH * D_MODEL
    gate_rows = gate_cols // (bsz * nt)
    assert gate_rows * bsz * nt == gate_cols and gate_rows % LANES == 0
    gate0 = _REF_OFFS[11]

    def rowblk(col):
        return pl.BlockSpec((rows, w), lambda b, t: (b * nt + t, col // w))

    def const(shape):
        return pl.BlockSpec(shape, lambda b, t: (0,) * len(shape))

    wt_spec = pl.BlockSpec((pl.Element(1), pl.Element(gate_rows), pl.Element(d)),
                           lambda b, t: (layer, pl.multiple_of(gate0 + (b * nt + t) * gate_rows, 8), 0))
    return pl.pallas_call(
        _conf_kernel,
        out_shape=(jax.ShapeDtypeStruct((bsz * seq, w), BF16), jax.ShapeDtypeStruct((d, gate_cols), BF16)),
        grid=(bsz, nt),
        in_specs=[rowblk(COL_CONF), rowblk(COL_CONF + w), const((CONF_CONV_WIDTH, w)), const((1, w)),
                  const((1, w)), const((1, w)), const((w, w)), const((1, w)), wt_spec],
        out_specs=(pl.BlockSpec((rows, w), lambda b, t: (b * nt + t, 0)),
                   pl.BlockSpec((d, gate_rows), lambda b, t: (0, b * nt + t))),
        scratch_shapes=[pltpu.VMEM((32 + rows, w), F32),
                        pltpu.VMEM((rows + 8 * ((CONF_CONV_WIDTH - 1) // 8), w), F32)],
        compiler_params=_params(("parallel", "arbitrary")), name="conformer_conv",
    )(proj, proj, conv_w, conv_b.reshape(1, w), ln_g.reshape(1, w), ln_b.reshape(1, w),
      pw_w.astype(BF16), pw_b.reshape(1, w), wt)


def _ssd_kernel(z_ref, xs_ref, bc_ref, small_ref, wx_ref, wbc_ref, bx_ref, bbc_ref, nexpa_ref, dtb_ref,
                dskip_ref, normw_ref, o_ref, xpad_ref, bcpad_ref, state_ref, y_ref, *, rows):
    t_first = pl.program_id(1) == 0

    @pl.when(t_first)
    def _():
        state_ref[...] = jnp.zeros(state_ref.shape, F32)

    xs = _silu(_causal_conv(xpad_ref, xs_ref[...], wx_ref, 8, t_first) + bx_ref[...])
    bc = _silu(_causal_conv(bcpad_ref, bc_ref[...], wbc_ref, 8, t_first) + bbc_ref[...])
    dt_all = _softplus(small_ref[...] + dtb_ref[...])
    a_all = nexpa_ref[...] * dt_all
    gs = SSD_GROUPS * SSD_STATE
    nch = rows // CHUNK
    pairs = SSD_HEADS // 2
    pairs_per_group = pairs // SSD_GROUPS

    er = lax.broadcasted_iota(jnp.int32, (LANES, SSD_WIDTH), 0)
    ec = lax.broadcasted_iota(jnp.int32, (LANES, SSD_WIDTH), 1)
    expand = jnp.where(er == SMALL_DT + (ec >> 6), 1.0, 0.0).astype(F32)

    def per_head_cols(v):
        return jnp.dot(v, expand, preferred_element_type=F32, precision=lax.Precision.HIGHEST)

    xdt = xs * per_head_cols(dt_all)
    lane = lax.broadcasted_iota(jnp.int32, (1, LANES), 1)
    low = lane < SSD_HEAD_DIM
    row2 = lax.broadcasted_iota(jnp.int32, (2 * CHUNK, 1), 0)
    l_idx = lax.broadcasted_iota(jnp.int32, (CHUNK, LANES), 0)
    s_idx = lax.broadcasted_iota(jnp.int32, (CHUNK, LANES), 1) & (CHUNK - 1)
    causal2 = l_idx >= s_idx

    lhs, rhs, xdec, bms, cms, e_ac, h_dec = [], [], [], [], [], [], []
    for ci in range(nch):
        r0 = ci * CHUNK
        acs = _chunk_cumsum(a_all[r0:r0 + CHUNK, :])
        ac_cols = per_head_cols(acs)
        acs_t = jnp.concatenate([acs, acs], axis=0).T
        a_last_cols = ac_cols[CHUNK - 1:CHUNK, :]
        xdec_c = xdt[r0:r0 + CHUNK, :] * jnp.exp(a_last_cols - ac_cols)
        e_ac_c = jnp.exp(ac_cols)
        cb2 = []
        for g in range(SSD_GROUPS):
            bm = bc[r0:r0 + CHUNK, g * SSD_STATE:(g + 1) * SSD_STATE]
            cm = bc[r0:r0 + CHUNK, gs + g * SSD_STATE:gs + (g + 1) * SSD_STATE]
            cb2.append(_bdot_nt(cm, jnp.concatenate([bm, bm], axis=0)))
        for p in range(pairs):
            g = p // pairs_per_group
            c0 = p * LANES
            l0 = SMALL_DT + 2 * p
            a_row = jnp.where(low, acs_t[l0:l0 + 1, :], acs_t[l0 + 1:l0 + 2, :])
            seg = _decay_matrix(ac_cols[:, c0:c0 + LANES], a_row, causal2)
            lhs.append(cb2[g] * seg)
            xp = xdt[r0:r0 + CHUNK, c0:c0 + LANES]
            rhs.append(jnp.concatenate([jnp.where(low, xp, 0.0), jnp.where(low, 0.0, xp)], axis=0))
            xdec.append(xdec_c[:, c0:c0 + LANES])
            bms.append(bc[r0:r0 + CHUNK, g * SSD_STATE:(g + 1) * SSD_STATE])
            cms.append(bc[r0:r0 + CHUNK, gs + g * SSD_STATE:gs + (g + 1) * SSD_STATE])
            e_ac.append(e_ac_c[:, c0:c0 + LANES])
            a_last = jnp.where(row2 < SSD_HEAD_DIM, acs[CHUNK - 1:CHUNK, l0:l0 + 1],
                               acs[CHUNK - 1:CHUNK, l0 + 1:l0 + 2])
            h_dec.append(jnp.exp(a_last))

    y_diag = _bmm(jnp.stack(lhs), jnp.stack(rhs))
    states = _bmm_tn(jnp.stack(xdec), jnp.stack(bms))
    cms, e_ac, h_dec = jnp.stack(cms), jnp.stack(e_ac), jnp.stack(h_dec)

    h = state_ref[...]
    for ci in range(nch):
        sl = slice(ci * pairs, (ci + 1) * pairs)
        y = y_diag[sl] + _bmm_nt(cms[sl], h) * e_ac[sl]
        h = h * h_dec[sl] + states[sl]
        for p in range(pairs):
            y_ref[ci * CHUNK:(ci + 1) * CHUNK, p * LANES:(p + 1) * LANES] = y[p]
    state_ref[...] = h

    y = y_ref[...] + dskip_ref[...] * xs
    y = y * _silu(z_ref[...])
    y = y * lax.rsqrt(jnp.mean(y * y, axis=-1, keepdims=True) + RMS_EPS) * normw_ref[...]
    o_ref[...] = y.astype(o_ref.dtype)


def _mamba2_ssd(proj, conv_w, conv_b, a_log, dt_bias, d_skip, norm_w, *, bsz, seq, rows=256):
    nt = seq // rows
    w = SSD_WIDTH
    nexpa = jnp.zeros((1, LANES), F32).at[0, SMALL_DT:SMALL_DT + SSD_HEADS].set(-jnp.exp(a_log))
    dtb = jnp.zeros((1, LANES), F32).at[0, SMALL_DT:SMALL_DT + SSD_HEADS].set(dt_bias)
    d_full = jnp.repeat(d_skip, SSD_HEAD_DIM).reshape(1, w)

    def rowblk(width, col):
        return pl.BlockSpec((rows, width), lambda b, t: (b * nt + t, col // width))

    def const(shape):
        return pl.BlockSpec(shape, lambda b, t: (0,) * len(shape))

    return pl.pallas_call(
        functools.partial(_ssd_kernel, rows=rows),
        out_shape=jax.ShapeDtypeStruct((bsz * seq, w), BF16),
        grid=(bsz, nt),
        in_specs=[rowblk(w, COL_Z), rowblk(w, COL_XS), rowblk(SSD_BC_WIDTH, COL_BC), rowblk(LANES, COL_SMALL),
                  const((SHORT_CONV, w)), const((SHORT_CONV, SSD_BC_WIDTH)), const((1, w)),
                  const((1, SSD_BC_WIDTH)), const((1, LANES)), const((1, LANES)), const((1, w)), const((1, w))],
        out_specs=pl.BlockSpec((rows, w), lambda b, t: (b * nt + t, 0)),
        scratch_shapes=[pltpu.VMEM((8 + rows, w), F32), pltpu.VMEM((8 + rows, SSD_BC_WIDTH), F32),
                        pltpu.VMEM((SSD_HEADS // 2, 2 * SSD_HEAD_DIM, SSD_STATE), F32),
                        pltpu.VMEM((rows, w), F32)],
        compiler_params=_params(("parallel", "arbitrary")), name="mamba2_ssd",
    )(proj, proj, proj, proj, conv_w[:, :w], conv_w[:, w:], conv_b[:w].reshape(1, w),
      conv_b[w:].reshape(1, SSD_BC_WIDTH), nexpa, dtb, d_full, norm_w.reshape(1, w))


PREP_TN = 512


def _regroup_kernel(wt_ref, *rest, small_at):
    o_ref = rest[-1]
    j = pl.program_id(1)

    @pl.when(j != small_at)
    def _():
        o_ref[...] = wt_ref[0].T.astype(BF16)

    if len(rest) == 2:
        @pl.when(j == small_at)
        def _():
            used = SMALL_DT + SSD_HEADS
            pad = jnp.zeros((PREP_TN - used, wt_ref.shape[2]), F32)
            rows = jnp.concatenate([wt_ref[0, 0:SMALL_DT, :], rest[0][0], pad], axis=0)
            o_ref[...] = rows.T.astype(BF16)


def _regroup_weights(wt, n_blocks, row_start, *, small_at=-1, dt_start=None, name):
    layers, _, d = wt.shape
    in_specs = [pl.BlockSpec((pl.Element(1), pl.Element(PREP_TN), pl.Element(d)),
                             lambda l, j: (l, pl.multiple_of(row_start(j), 8), 0))]
    args = [wt]
    if dt_start is not None:
        in_specs.append(pl.BlockSpec((pl.Element(1), pl.Element(SSD_HEADS), pl.Element(d)),
                                     lambda l, j: (l, dt_start, 0)))
        args.append(wt)
    return pl.pallas_call(
        functools.partial(_regroup_kernel, small_at=small_at),
        out_shape=jax.ShapeDtypeStruct((layers, d, n_blocks * PREP_TN), BF16),
        grid=(layers, n_blocks), in_specs=in_specs,
        out_specs=pl.BlockSpec((None, d, PREP_TN), lambda l, j: (l, 0, j)),
        compiler_params=_params(("parallel", "parallel")), name=name)(*args)


def _rearranged_in_weights(w_in):
    o = _REF_OFFS
    wt = jnp.swapaxes(w_in, 1, 2)
    q0, conf0, kv, small = (c // PREP_TN for c in (COL_SWA_Q, COL_CONF, COL_SWA_K, COL_SMALL))
    assert o[2] == COL_SWA_Q and o[3] - o[2] == GDN_HEADS == SMALL_B and o[4] - o[2] == SMALL_DT

    def mixer_rows(j):
        c = j * PREP_TN
        return jnp.where(j < q0, c, jnp.where(j < conf0, c + (o[4] - COL_SWA_Q),
                         jnp.where(j < kv, c + (o[7] - COL_CONF), jnp.where(j < small, o[5], o[2]))))

    w_small = _regroup_weights(wt, PROJ_COLS // PREP_TN, mixer_rows, small_at=small, dt_start=o[10],
                               name="regroup_mixer_weights")
    return w_small, wt


def _rotary_tables(seq):
    inv_freq = ROPE_THETA ** (-jnp.arange(0, SWA_HEAD_DIM, 2, dtype=F32) / SWA_HEAD_DIM)
    ang = jnp.arange(seq, dtype=F32)[:, None] * inv_freq[None, :]
    cos, sin = jnp.cos(ang), jnp.sin(ang)
    return jnp.concatenate([cos, cos], axis=1), jnp.concatenate([-sin, sin], axis=1)


def kernel(x, w_in, gdn_conv_w, gdn_a_log, gdn_dt_bias, gdn_norm_w, swa_sinks, conf_conv_w, conf_conv_b, conf_ln_g, conf_ln_b, conf_pw_w, conf_pw_b, ssd_conv_w, ssd_conv_b, ssd_a_log, ssd_dt_bias, ssd_d, ssd_norm_w, w_branch, w_out, ln1_g, ln1_b, w_gate_up, w_down, ln2_g, ln2_b):
    bsz, seq, d = x.shape
    m = bsz * seq
    cos_full, sin_signed = _rotary_tables(seq)
    xf = x.reshape(m, d)
    xb = xf.astype(BF16)
    w_small, w_in_t = _rearranged_in_weights(w_in)
    for i in range(DEPTH):
        proj = _matmul(xb, w_small, i, tm=512, tn=1536, out_dtype=F32, name="in_proj")
        o_a = _gated_deltanet(proj, gdn_conv_w[i], gdn_a_log[i], gdn_dt_bias[i], gdn_norm_w[i],
                              bsz=bsz, seq=seq)
        o_b = _sliding_window_attention(proj, swa_sinks[i], cos_full, sin_signed, bsz=bsz, seq=seq)
        o_c, w_gates = _conformer_conv(proj, conf_conv_w[i], conf_conv_b[i], conf_ln_g[i], conf_ln_b[i],
                                       conf_pw_w[i], conf_pw_b[i], w_in_t, i, bsz=bsz, seq=seq)
        o_d = _mamba2_ssd(proj, ssd_conv_w[i], ssd_conv_b[i], ssd_a_log[i], ssd_dt_bias[i], ssd_d[i],
                          ssd_norm_w[i], bsz=bsz, seq=seq)
        merged, w_out_b = _merge(xb, (o_a, o_b, o_c, o_d), w_gates, w_branch, w_out, i, tm=512, tn=256)
        s1 = _matmul(merged, w_out_b[None], 0, tm=512, tn=1024, out_dtype=F32, resid=xf, name="out_proj")
        xf, xb = _layer_norm(s1, ln1_g[i], ln1_b[i], tr=256)
        act, w_down_b = _swiglu(xb, w_gate_up, w_down, i, tm=1024, tn=256)
        s2 = _matmul(act, w_down_b[None], 0, tm=512, tn=512, out_dtype=F32, resid=xf, name="down_proj")
        xf, xb = _layer_norm(s2, ln2_g[i], ln2_b[i], tr=256)
    return xf.reshape(bsz, seq, d)
```

```python
import functools

import jax
import jax.numpy as jnp
import numpy as np
from jax import lax
from jax.experimental import pallas as pl
from jax.experimental.pallas import tpu as pltpu

F32 = jnp.float32
BF16 = jnp.bfloat16

D_MODEL = 4096
DEPTH = 2
N_BRANCH = 4
BRANCH_WIDTH = D_MODEL // 4

GDN_HEAD_DIM = 128
GDN_HEADS = BRANCH_WIDTH // GDN_HEAD_DIM
GDN_WIDTH = GDN_HEADS * GDN_HEAD_DIM
CHUNK = 64
SHORT_CONV = 4

SWA_HEAD_DIM = 128
SWA_Q_HEADS = BRANCH_WIDTH // SWA_HEAD_DIM
SWA_KV_HEADS = SWA_Q_HEADS // 4
SWA_REP = SWA_Q_HEADS // SWA_KV_HEADS
SWA_WINDOW = 128
ROPE_THETA = 10000.0

CONF_WIDTH = BRANCH_WIDTH
CONF_CONV_WIDTH = 31

SSD_HEAD_DIM = 64
SSD_WIDTH = BRANCH_WIDTH
SSD_HEADS = SSD_WIDTH // SSD_HEAD_DIM
SSD_GROUPS = 2
SSD_HEADS_PER_GROUP = SSD_HEADS // SSD_GROUPS
SSD_STATE = 128
SSD_BC_WIDTH = 2 * SSD_GROUPS * SSD_STATE

FFN_HIDDEN = -(-8 * D_MODEL // (3 * 256)) * 256

ALPHA = (2.0 * DEPTH) ** 0.25
LN_EPS = 1e-5
RMS_EPS = 1e-6

LANES = 128
VMEM_LIMIT = 52 * 1024 * 1024

COL_QKV = 0
COL_GATE = COL_QKV + 3 * GDN_WIDTH
COL_SWA_Q = COL_GATE + GDN_WIDTH
COL_CONF = COL_SWA_Q + BRANCH_WIDTH
COL_Z = COL_CONF + 2 * CONF_WIDTH
COL_XS = COL_Z + SSD_WIDTH
COL_BC = COL_XS + SSD_WIDTH
COL_SWA_K = COL_BC + SSD_BC_WIDTH
COL_SWA_V = COL_SWA_K + SWA_KV_HEADS * SWA_HEAD_DIM
COL_SMALL = COL_SWA_V + SWA_KV_HEADS * SWA_HEAD_DIM
SMALL_A = 0
SMALL_B = GDN_HEADS
SMALL_DT = 2 * GDN_HEADS
PROJ_COLS = 10752

_REF_SPLITS = (3 * GDN_WIDTH, GDN_WIDTH, GDN_HEADS, GDN_HEADS, BRANCH_WIDTH, 256, 256,
               2 * CONF_WIDTH, SSD_WIDTH, SSD_WIDTH + SSD_BC_WIDTH, SSD_HEADS, N_BRANCH * D_MODEL)
_REF_OFFS = np.concatenate([[0], np.cumsum(_REF_SPLITS)]).tolist()


def _params(sem, vmem=VMEM_LIMIT):
    return pltpu.CompilerParams(dimension_semantics=sem, vmem_limit_bytes=vmem)


def _bdot(a, b):
    return jnp.dot(a.astype(BF16), b.astype(BF16), preferred_element_type=F32)


def _bdot_nt(a, b):
    return lax.dot_general(a.astype(BF16), b.astype(BF16), (((1,), (1,)), ((), ())),
                           preferred_element_type=F32)


def _bdot_tn(a, b):
    return lax.dot_general(a.astype(BF16), b.astype(BF16), (((0,), (0,)), ((), ())),
                           preferred_element_type=F32)


def _silu(x):
    return x * jax.nn.sigmoid(x)


def _softplus(x):
    return jnp.maximum(x, 0.0) + jnp.log(1.0 + jnp.exp(-jnp.abs(x)))


def _cast_weights_once(w_refs, wb_refs):
    @pl.when(pl.program_id(1) == 0)
    def _():
        for w_ref, wb_ref in zip(w_refs, wb_refs):
            wb_ref[...] = w_ref[...].astype(BF16)


def _mm_kernel(x_ref, w_ref, *rest, has_resid):
    r_ref, o_ref = rest if has_resid else (None, rest[0])
    acc = jnp.dot(x_ref[...], w_ref[...], preferred_element_type=F32)
    if has_resid:
        acc = ALPHA * r_ref[...] + acc
    o_ref[...] = acc.astype(o_ref.dtype)


def _matmul(x, w, *, tm, tn, out_dtype, resid=None, name):
    m, k = x.shape
    n = w.shape[1]
    assert x.dtype == BF16 and w.dtype == BF16
    in_specs = [pl.BlockSpec((tm, k), lambda j, i: (i, 0)),
                pl.BlockSpec((k, tn), lambda j, i: (0, j))]
    args = [x, w]
    if resid is not None:
        in_specs.append(pl.BlockSpec((tm, tn), lambda j, i: (i, j)))
        args.append(resid)
    return pl.pallas_call(
        functools.partial(_mm_kernel, has_resid=resid is not None),
        out_shape=jax.ShapeDtypeStruct((m, n), out_dtype),
        grid=(n // tn, m // tm), in_specs=in_specs,
        out_specs=pl.BlockSpec((tm, tn), lambda j, i: (i, j)),
        compiler_params=_params(("parallel", "parallel")), name=name)(*args)


def _swiglu_kernel(x_ref, wg0_ref, wg1_ref, wu0_ref, wu1_ref, wd_ref, o_ref, wdb_ref, wgb_ref, wub_ref):
    half = wg0_ref.shape[0]
    _cast_weights_once([wg0_ref, wg1_ref, wu0_ref, wu1_ref],
                       [wgb_ref.at[0:half], wgb_ref.at[half:2 * half], wub_ref.at[0:half], wub_ref.at[half:2 * half]])
    x = x_ref[...]
    g = jnp.dot(x, wgb_ref[...], preferred_element_type=F32)
    u = jnp.dot(x, wub_ref[...], preferred_element_type=F32)
    o_ref[...] = (_silu(g) * u).astype(o_ref.dtype)
    wdb_ref[...] = wd_ref[...].astype(BF16)


def _swiglu(x, w_gate_up, w_down, layer, *, tm, tn):
    m, k = x.shape
    hidden = w_gate_up.shape[2] // 2
    nb = hidden // tn
    n_inner = m // tm
    d_out = w_down.shape[2]
    wd_rows = hidden // (nb * n_inner)
    assert wd_rows * nb * n_inner == hidden and wd_rows % 16 == 0

    def weight_half(which, up, order):
        def index(j, i):
            ahead = jnp.minimum(j + jnp.where(i > order, 1, 0), nb - 1)
            return (layer, which, ahead + (nb if up else 0))
        return pl.BlockSpec((None, k // 2, tn), index)

    return pl.pallas_call(
        _swiglu_kernel,
        out_shape=(jax.ShapeDtypeStruct((m, hidden), BF16), jax.ShapeDtypeStruct((hidden, d_out), BF16)),
        grid=(nb, n_inner),
        in_specs=[pl.BlockSpec((tm, k), lambda j, i: (i, 0)),
                  weight_half(0, False, 0), weight_half(1, False, 1),
                  weight_half(0, True, 2), weight_half(1, True, 3),
                  pl.BlockSpec((None, wd_rows, d_out), lambda j, i: (layer, j * n_inner + i, 0))],
        out_specs=(pl.BlockSpec((tm, tn), lambda j, i: (i, j)),
                   pl.BlockSpec((wd_rows, d_out), lambda j, i: (j * n_inner + i, 0))),
        scratch_shapes=[pltpu.VMEM((k, tn), BF16), pltpu.VMEM((k, tn), BF16)],
        compiler_params=_params(("arbitrary", "arbitrary")), name="swiglu",
    )(x, *([w_gate_up] * 4), w_down)


def _merge_kernel(x_ref, oa_ref, ob_ref, oc_ref, od_ref,
                  wg0, wg1, wg2, wg3, wb0, wb1, wb2, wb3, wo_ref, o_ref, wob_ref, wbb0, wbb1, wbb2, wbb3):
    _cast_weights_once([wb0, wb1, wb2, wb3], [wbb0, wbb1, wbb2, wbb3])
    wob_ref[...] = wo_ref[...].astype(BF16)
    x = x_ref[...]
    acc = None
    for o_r, wg, wb in ((oa_ref, wg0, wbb0), (ob_ref, wg1, wbb1), (oc_ref, wg2, wbb2), (od_ref, wg3, wbb3)):
        gate = jax.nn.sigmoid(jnp.dot(x, wg[...], preferred_element_type=F32))
        up = jnp.dot(o_r[...], wb[...], preferred_element_type=F32)
        acc = gate * up if acc is None else acc + gate * up
    o_ref[...] = acc.astype(o_ref.dtype)


def _merge(x, branches, w_gates, w_branch, w_out, layer, *, tm, tn):
    m, k = x.shape
    d = w_branch.shape[3]
    nb = d // tn
    bw = w_branch.shape[2]
    n_inner = m // tm
    wo_rows = w_out.shape[1] // (nb * n_inner)
    assert wo_rows * nb * n_inner == w_out.shape[1] and wo_rows % 16 == 0
    in_specs = [pl.BlockSpec((tm, k), lambda j, i: (i, 0))]
    in_specs += [pl.BlockSpec((tm, bw), lambda j, i: (i, 0)) for _ in range(N_BRANCH)]
    in_specs += [pl.BlockSpec((k, tn), functools.partial(lambda j, i, g: (0, g * nb + j), g=g))
                 for g in range(N_BRANCH)]
    in_specs += [pl.BlockSpec((None, None, bw, tn), functools.partial(lambda j, i, g: (layer, g, 0, j), g=g))
                 for g in range(N_BRANCH)]
    in_specs.append(pl.BlockSpec((None, wo_rows, w_out.shape[2]), lambda j, i: (layer, j * n_inner + i, 0)))
    return pl.pallas_call(
        _merge_kernel,
        out_shape=(jax.ShapeDtypeStruct((m, d), BF16), jax.ShapeDtypeStruct(w_out.shape[1:], BF16)),
        grid=(nb, n_inner), in_specs=in_specs,
        out_specs=(pl.BlockSpec((tm, tn), lambda j, i: (i, j)),
                   pl.BlockSpec((wo_rows, w_out.shape[2]), lambda j, i: (j * n_inner + i, 0))),
        scratch_shapes=[pltpu.VMEM((bw, tn), BF16) for _ in range(N_BRANCH)],
        compiler_params=_params(("parallel", "arbitrary")), name="merge",
    )(x, *branches, *([w_gates] * N_BRANCH), *([w_branch] * N_BRANCH), w_out)


def _ln_kernel(s_ref, g_ref, b_ref, of_ref, ob_ref):
    s = s_ref[...]
    mu = jnp.mean(s, axis=-1, keepdims=True)
    c = s - mu
    var = jnp.mean(c * c, axis=-1, keepdims=True)
    y = c * lax.rsqrt(var + LN_EPS) * g_ref[...] + b_ref[...]
    of_ref[...] = y
    ob_ref[...] = y.astype(BF16)


def _layer_norm(s, g, b, *, tr):
    m, d = s.shape
    row = pl.BlockSpec((tr, d), lambda i: (i, 0))
    vec = pl.BlockSpec((1, d), lambda i: (0, 0))
    return pl.pallas_call(
        _ln_kernel,
        out_shape=(jax.ShapeDtypeStruct((m, d), F32), jax.ShapeDtypeStruct((m, d), BF16)),
        grid=(m // tr,), in_specs=[row, vec, vec], out_specs=(row, row),
        compiler_params=_params(("parallel",)), name="layer_norm")(s, g.reshape(1, d), b.reshape(1, d))


def _causal_conv(pad_ref, x, w_ref, halo, t_first, shift_ref=None):
    rows = x.shape[0]
    k = w_ref.shape[0]

    @pl.when(t_first)
    def _():
        pad_ref[0:halo, :] = jnp.zeros((halo, x.shape[1]), F32)

    pad_ref[halo:halo + rows, :] = x
    acc = None
    sublanes = 8
    first = halo - (k - 1)
    for phase in range(min(sublanes, k)):
        taps = list(range(phase, k, sublanes))
        lo = first + taps[0]
        if shift_ref is not None:
            span = rows + taps[-1] - taps[0]
            shift_ref[0:span, :] = pad_ref[lo:lo + span, :]
        for j in taps:
            if shift_ref is not None:
                window = shift_ref[j - taps[0]:j - taps[0] + rows, :]
            else:
                window = pad_ref[first + j:first + j + rows, :]
            term = w_ref[j:j + 1, :] * window
            acc = term if acc is None else acc + term
    pad_ref[0:halo, :] = x[rows - halo:rows, :]
    return acc


def _chunk_cumsum(x):
    r = lax.broadcasted_iota(jnp.int32, (CHUNK, CHUNK), 0)
    c = lax.broadcasted_iota(jnp.int32, (CHUNK, CHUNK), 1)
    tril = jnp.where(r >= c, 1.0, 0.0).astype(F32)
    return jnp.dot(tril, x, preferred_element_type=F32, precision=lax.Precision.HIGHEST)


def _bmm(a, b):
    return lax.dot_general(a.astype(BF16), b.astype(BF16), (((2,), (1,)), ((0,), (0,))),
                           preferred_element_type=F32)


def _bmm_nt(a, b):
    return lax.dot_general(a.astype(BF16), b.astype(BF16), (((2,), (2,)), ((0,), (0,))),
                           preferred_element_type=F32)


def _bmm_tn(a, b):
    return lax.dot_general(a.astype(BF16), b.astype(BF16), (((1,), (1,)), ((0,), (0,))),
                           preferred_element_type=F32)


def _unit_lower_inverse(a):
    r = lax.broadcasted_iota(jnp.int32, (1, CHUNK, CHUNK), 1)
    c = lax.broadcasted_iota(jnp.int32, (1, CHUNK, CHUNK), 2)
    same16 = (r >> 4) == (c >> 4)
    same32 = (r >> 5) == (c >> 5)
    eye = jnp.where(r == c, 1.0, 0.0).astype(F32)
    ad = jnp.where(same16, a, 0.0)
    o1 = jnp.where(same32, a, 0.0) - ad
    o2 = jnp.where(same32, 0.0, a)
    a2 = _bmm(ad, ad)
    d = eye - ad
    a4 = _bmm(a2, a2)
    d = d + _bmm(d, a2)
    a8 = _bmm(a4, a4)
    d = d + _bmm(d, a4)
    d = d + _bmm(d, a8)
    d = d - _bmm(d, _bmm(o1, d))
    d = d - _bmm(d, _bmm(o2, d))
    return d


def _causal_masks(lead=()):
    shape = lead + (CHUNK, CHUNK)
    r = lax.broadcasted_iota(jnp.int32, shape, len(lead))
    c = lax.broadcasted_iota(jnp.int32, shape, len(lead) + 1)
    return r >= c, r > c


def _decay_matrix(col, row, causal):
    return jnp.where(causal, jnp.exp(jnp.where(causal, col - row, 0.0)), 0.0)


def _gdn_kernel(qkv_ref, gate_ref, small_ref, convw_ref, nexpa_ref, dtb_ref, normw_ref,
                o_ref, pad_ref, state_ref, *, rows):
    t_first = pl.program_id(1) == 0

    @pl.when(t_first)
    def _():
        state_ref[...] = jnp.zeros(state_ref.shape, F32)

    y = _silu(_causal_conv(pad_ref, qkv_ref[...], convw_ref, 8, t_first))
    small = small_ref[...]
    g_all = nexpa_ref[...] * _softplus(small + dtb_ref[...])
    beta_all = jax.nn.sigmoid(small)
    causal, strict = _causal_masks((1,))
    nch = rows // CHUNK
    gcums = [_chunk_cumsum(g_all[ci * CHUNK:(ci + 1) * CHUNK, :]) for ci in range(nch)]
    gcums_t = [g.T for g in gcums]

    def stack(fn):
        return jnp.stack([fn(ci, h) for ci in range(nch) for h in range(GDN_HEADS)], axis=0)

    def head_cols(base):
        return stack(lambda ci, h: y[ci * CHUNK:(ci + 1) * CHUNK,
                                     base + h * GDN_HEAD_DIM:base + (h + 1) * GDN_HEAD_DIM])

    q, k, v = head_cols(0), head_cols(GDN_WIDTH), head_cols(2 * GDN_WIDTH)
    q = q * lax.rsqrt(jnp.sum(q * q, axis=-1, keepdims=True) + RMS_EPS) * (GDN_HEAD_DIM ** -0.5)
    k = k * lax.rsqrt(jnp.sum(k * k, axis=-1, keepdims=True) + RMS_EPS)
    beta = stack(lambda ci, h: beta_all[ci * CHUNK:(ci + 1) * CHUNK, SMALL_B + h:SMALL_B + h + 1])
    gc = stack(lambda ci, h: gcums[ci][:, SMALL_A + h:SMALL_A + h + 1])
    gr = stack(lambda ci, h: gcums_t[ci][SMALL_A + h:SMALL_A + h + 1, :])
    g_last = gc[:, CHUNK - 1:CHUNK, :]
    decay = _decay_matrix(gc, gr, causal)
    kb = k * beta
    t_mat = _unit_lower_inverse(jnp.where(strict, _bmm_nt(kb, k) * decay, 0.0))
    e_gc = jnp.exp(gc)
    u = _bmm(t_mat, v * beta)
    w = _bmm(t_mat, kb * e_gc)
    qk = _bmm_nt(q, k) * decay
    q_dec = q * e_gc
    k_dec = k * jnp.exp(g_last - gc)
    chunk_decay = jnp.exp(g_last)

    norm_w = normw_ref[...]
    state = state_ref[...]
    for ci in range(nch):
        sl = slice(ci * GDN_HEADS, (ci + 1) * GDN_HEADS)
        v_new = u[sl] - _bmm(w[sl], state)
        o = _bmm(q_dec[sl], state) + _bmm(qk[sl], v_new)
        state = state * chunk_decay[sl] + _bmm_tn(k_dec[sl], v_new)
        o = o * lax.rsqrt(jnp.mean(o * o, axis=-1, keepdims=True) + RMS_EPS) * norm_w
        for h in range(GDN_HEADS):
            c0 = h * GDN_HEAD_DIM
            gate = gate_ref[ci * CHUNK:(ci + 1) * CHUNK, c0:c0 + GDN_HEAD_DIM]
            o_ref[ci * CHUNK:(ci + 1) * CHUNK, c0:c0 + GDN_HEAD_DIM] = (o[h] * _silu(gate)).astype(o_ref.dtype)
    state_ref[...] = state


def _gated_deltanet(proj, conv_w, a_log, dt_bias, norm_w, *, bsz, seq, rows=256):
    nt = seq // rows
    nexpa = jnp.zeros((1, LANES), F32).at[0, SMALL_A:SMALL_A + GDN_HEADS].set(-jnp.exp(a_log))
    dtb = jnp.zeros((1, LANES), F32).at[0, SMALL_A:SMALL_A + GDN_HEADS].set(dt_bias)

    def rowblk(width, col):
        return pl.BlockSpec((rows, width), lambda b, t: (b * nt + t, col // width))

    def const(shape):
        return pl.BlockSpec(shape, lambda b, t: (0,) * len(shape))

    return pl.pallas_call(
        functools.partial(_gdn_kernel, rows=rows),
        out_shape=jax.ShapeDtypeStruct((bsz * seq, GDN_WIDTH), BF16),
        grid=(bsz, nt),
        in_specs=[rowblk(3 * GDN_WIDTH, COL_QKV), rowblk(GDN_WIDTH, COL_GATE), rowblk(LANES, COL_SMALL),
                  const((SHORT_CONV, 3 * GDN_WIDTH)), const((1, LANES)), const((1, LANES)),
                  const((1, GDN_HEAD_DIM))],
        out_specs=pl.BlockSpec((rows, GDN_WIDTH), lambda b, t: (b * nt + t, 0)),
        scratch_shapes=[pltpu.VMEM((8 + rows, 3 * GDN_WIDTH), F32),
                        pltpu.VMEM((GDN_HEADS, GDN_HEAD_DIM, GDN_HEAD_DIM), F32)],
        compiler_params=_params(("parallel", "arbitrary")), name="gated_deltanet",
    )(proj, proj, proj, conv_w, nexpa, dtb, norm_w.reshape(1, GDN_HEAD_DIM))


def _swa_kernel(sinks_ref, q_ref, kc_ref, kp_ref, vc_ref, vp_ref, cos_ref, sin_ref, cosp_ref, sinp_ref,
                o_ref):
    n = pl.program_id(1)
    blk = SWA_WINDOW
    cos, sin = cos_ref[...], sin_ref[...]
    cosp, sinp = cosp_ref[...], sinp_ref[...]

    def rope(x, c, s):
        return x * c + pltpu.roll(x, SWA_HEAD_DIM // 2, axis=1) * s

    rows = SWA_REP * blk
    qi = lax.broadcasted_iota(jnp.int32, (rows, 2 * blk), 0) & (blk - 1)
    kj = lax.broadcasted_iota(jnp.int32, (rows, 2 * blk), 1)
    dist = qi + blk - kj
    in_window = jnp.where(dist >= 0, jnp.where(dist < SWA_WINDOW, 1, 0), 0)
    has_prev = jnp.where(kj >= blk, 1, jnp.where(n > 0, 1, 0))
    mask = (in_window * has_prev) > 0
    head_of_row = lax.broadcasted_iota(jnp.int32, (rows, 1), 0) >> 7

    for g in range(SWA_KV_HEADS):
        d0 = g * SWA_HEAD_DIM
        q_rows = [rope(q_ref[:, (g * SWA_REP + j) * SWA_HEAD_DIM:(g * SWA_REP + j + 1) * SWA_HEAD_DIM], cos, sin)
                  for j in range(SWA_REP)]
        qs = jnp.concatenate(q_rows, axis=0)
        kband = jnp.concatenate([rope(kp_ref[:, d0:d0 + SWA_HEAD_DIM], cosp, sinp),
                                 rope(kc_ref[:, d0:d0 + SWA_HEAD_DIM], cos, sin)], axis=0)
        vband = jnp.concatenate([vp_ref[:, d0:d0 + SWA_HEAD_DIM], vc_ref[:, d0:d0 + SWA_HEAD_DIM]], axis=0)
        s = _bdot_nt(qs, kband) * (SWA_HEAD_DIM ** -0.5)
        s = jnp.where(mask, s, -jnp.inf)
        sink = jnp.zeros((rows, 1), F32)
        for j in range(SWA_REP):
            sink = jnp.where(head_of_row == j, sinks_ref[g * SWA_REP + j], sink)
        m = jnp.maximum(jnp.max(s, axis=-1, keepdims=True), sink)
        p = jnp.exp(s - m)
        denom = jnp.sum(p, axis=-1, keepdims=True) + jnp.exp(sink - m)
        o = _bdot(p / denom, vband)
        for j in range(SWA_REP):
            c0 = (g * SWA_REP + j) * SWA_HEAD_DIM
            o_ref[:, c0:c0 + SWA_HEAD_DIM] = o[j * blk:(j + 1) * blk, :].astype(o_ref.dtype)


def _sliding_window_attention(proj, sinks, cos_full, sin_signed, *, bsz, seq):
    blk = SWA_WINDOW
    nb = seq // blk
    kvw = SWA_KV_HEADS * SWA_HEAD_DIM

    def cur(width, col):
        return pl.BlockSpec((blk, width), lambda b, n: (b * nb + n, col // width))

    def prev(width, col):
        return pl.BlockSpec((blk, width), lambda b, n: (b * nb + jnp.maximum(n - 1, 0), col // width))

    tab_cur = pl.BlockSpec((blk, SWA_HEAD_DIM), lambda b, n: (n, 0))
    tab_prev = pl.BlockSpec((blk, SWA_HEAD_DIM), lambda b, n: (jnp.maximum(n - 1, 0), 0))
    return pl.pallas_call(
        _swa_kernel, out_shape=jax.ShapeDtypeStruct((bsz * seq, BRANCH_WIDTH), BF16),
        grid=(bsz, nb),
        in_specs=[pl.BlockSpec(memory_space=pltpu.SMEM),
                  cur(BRANCH_WIDTH, COL_SWA_Q), cur(kvw, COL_SWA_K), prev(kvw, COL_SWA_K),
                  cur(kvw, COL_SWA_V), prev(kvw, COL_SWA_V), tab_cur, tab_cur, tab_prev, tab_prev],
        out_specs=pl.BlockSpec((blk, BRANCH_WIDTH), lambda b, n: (b * nb + n, 0)),
        compiler_params=_params(("parallel", "parallel")), name="sliding_window_attention",
    )(sinks, proj, proj, proj, proj, proj, cos_full, sin_signed, cos_full, sin_signed)


def _conf_kernel(a_ref, gt_ref, convw_ref, convb_ref, lng_ref, lnb_ref, pww_ref, pwb_ref, wt_ref,
                 o_ref, wg_ref, pad_ref, shift_ref):
    wg_ref[...] = wt_ref[0].T.astype(BF16)
    t_first = pl.program_id(1) == 0
    h = a_ref[...] * jax.nn.sigmoid(gt_ref[...])
    h = _causal_conv(pad_ref, h, convw_ref, 32, t_first, shift_ref) + convb_ref[...]
    mu = jnp.mean(h, axis=-1, keepdims=True)
    c = h - mu
    var = jnp.mean(c * c, axis=-1, keepdims=True)
    h = _silu(c * lax.rsqrt(var + LN_EPS) * lng_ref[...] + lnb_ref[...])
    o_ref[...] = (jnp.dot(h.astype(BF16), pww_ref[...], preferred_element_type=F32)
                  + pwb_ref[...]).astype(o_ref.dtype)


def _conformer_conv(proj, conv_w, conv_b, ln_g, ln_b, pw_w, pw_b, wt, layer, *, bsz, seq, rows=256):
    nt = seq // rows
    w = CONF_WIDTH
    d = wt.shape[2]
    gate_cols = N_BRANCH * D_MODEL
    gate_rows = gate_cols // (bsz * nt)
    assert gate_rows * bsz * nt == gate_cols and gate_rows % LANES == 0
    gate0 = _REF_OFFS[11]

    def rowblk(col):
        return pl.BlockSpec((rows, w), lambda b, t: (b * nt + t, col // w))

    def const(shape):
        return pl.BlockSpec(shape, lambda b, t: (0,) * len(shape))

    wt_spec = pl.BlockSpec((pl.Element(1), pl.Element(gate_rows), pl.Element(d)),
                           lambda b, t: (layer, pl.multiple_of(gate0 + (b * nt + t) * gate_rows, 8), 0))
    return pl.pallas_call(
        _conf_kernel,
        out_shape=(jax.ShapeDtypeStruct((bsz * seq, w), BF16), jax.ShapeDtypeStruct((d, gate_cols), BF16)),
        grid=(bsz, nt),
        in_specs=[rowblk(COL_CONF), rowblk(COL_CONF + w), const((CONF_CONV_WIDTH, w)), const((1, w)),
                  const((1, w)), const((1, w)), const((w, w)), const((1, w)), wt_spec],
        out_specs=(pl.BlockSpec((rows, w), lambda b, t: (b * nt + t, 0)),
                   pl.BlockSpec((d, gate_rows), lambda b, t: (0, b * nt + t))),
        scratch_shapes=[pltpu.VMEM((32 + rows, w), F32),
                        pltpu.VMEM((rows + 8 * ((CONF_CONV_WIDTH - 1) // 8), w), F32)],
        compiler_params=_params(("parallel", "arbitrary")), name="conformer_conv",
    )(proj, proj, conv_w, conv_b.reshape(1, w), ln_g.reshape(1, w), ln_b.reshape(1, w),
      pw_w.astype(BF16), pw_b.reshape(1, w), wt)


def _ssd_kernel(z_ref, xs_ref, bc_ref, small_ref, wx_ref, wbc_ref, bx_ref, bbc_ref, nexpa_ref, dtb_ref,
                dskip_ref, normw_ref, *rest, rows, regroup):
    if regroup:
        wt_ref, wdt_ref, o_ref, wsmall_ref, xpad_ref, bcpad_ref, state_ref, y_ref = rest
        step = pl.program_id(0) * pl.num_programs(1) + pl.program_id(1)
        _regroup_mixer_block(step, wt_ref, wdt_ref, wsmall_ref)
    else:
        o_ref, xpad_ref, bcpad_ref, state_ref, y_ref = rest
    t_first = pl.program_id(1) == 0

    @pl.when(t_first)
    def _():
        state_ref[...] = jnp.zeros(state_ref.shape, F32)

    xs = _silu(_causal_conv(xpad_ref, xs_ref[...], wx_ref, 8, t_first) + bx_ref[...])
    bc = _silu(_causal_conv(bcpad_ref, bc_ref[...], wbc_ref, 8, t_first) + bbc_ref[...])
    dt_all = _softplus(small_ref[...] + dtb_ref[...])
    a_all = nexpa_ref[...] * dt_all
    gs = SSD_GROUPS * SSD_STATE
    nch = rows // CHUNK
    pairs = SSD_HEADS // 2
    pairs_per_group = pairs // SSD_GROUPS

    er = lax.broadcasted_iota(jnp.int32, (LANES, SSD_WIDTH), 0)
    ec = lax.broadcasted_iota(jnp.int32, (LANES, SSD_WIDTH), 1)
    expand = jnp.where(er == SMALL_DT + (ec >> 6), 1.0, 0.0).astype(F32)

    def per_head_cols(v):
        return jnp.dot(v, expand, preferred_element_type=F32, precision=lax.Precision.HIGHEST)

    xdt = xs * per_head_cols(dt_all)
    lane = lax.broadcasted_iota(jnp.int32, (1, LANES), 1)
    low = lane < SSD_HEAD_DIM
    row2 = lax.broadcasted_iota(jnp.int32, (2 * CHUNK, 1), 0)
    l_idx = lax.broadcasted_iota(jnp.int32, (CHUNK, LANES), 0)
    s_idx = lax.broadcasted_iota(jnp.int32, (CHUNK, LANES), 1) & (CHUNK - 1)
    causal2 = l_idx >= s_idx

    lhs, rhs, xdec, bms, cms, e_ac, h_dec = [], [], [], [], [], [], []
    for ci in range(nch):
        r0 = ci * CHUNK
        acs = _chunk_cumsum(a_all[r0:r0 + CHUNK, :])
        ac_cols = per_head_cols(acs)
        acs_t = jnp.concatenate([acs, acs], axis=0).T
        a_last_cols = ac_cols[CHUNK - 1:CHUNK, :]
        xdec_c = xdt[r0:r0 + CHUNK, :] * jnp.exp(a_last_cols - ac_cols)
        e_ac_c = jnp.exp(ac_cols)
        cb2 = []
        for g in range(SSD_GROUPS):
            bm = bc[r0:r0 + CHUNK, g * SSD_STATE:(g + 1) * SSD_STATE]
            cm = bc[r0:r0 + CHUNK, gs + g * SSD_STATE:gs + (g + 1) * SSD_STATE]
            cb2.append(_bdot_nt(cm, jnp.concatenate([bm, bm], axis=0)))
        for p in range(pairs):
            g = p // pairs_per_group
            c0 = p * LANES
            l0 = SMALL_DT + 2 * p
            a_row = jnp.where(low, acs_t[l0:l0 + 1, :], acs_t[l0 + 1:l0 + 2, :])
            seg = _decay_matrix(ac_cols[:, c0:c0 + LANES], a_row, causal2)
            lhs.append(cb2[g] * seg)
            xp = xdt[r0:r0 + CHUNK, c0:c0 + LANES]
            rhs.append(jnp.concatenate([jnp.where(low, xp, 0.0), jnp.where(low, 0.0, xp)], axis=0))
            xdec.append(xdec_c[:, c0:c0 + LANES])
            bms.append(bc[r0:r0 + CHUNK, g * SSD_STATE:(g + 1) * SSD_STATE])
            cms.append(bc[r0:r0 + CHUNK, gs + g * SSD_STATE:gs + (g + 1) * SSD_STATE])
            e_ac.append(e_ac_c[:, c0:c0 + LANES])
            a_last = jnp.where(row2 < SSD_HEAD_DIM, acs[CHUNK - 1:CHUNK, l0:l0 + 1],
                               acs[CHUNK - 1:CHUNK, l0 + 1:l0 + 2])
            h_dec.append(jnp.exp(a_last))

    y_diag = _bmm(jnp.stack(lhs), jnp.stack(rhs))
    states = _bmm_tn(jnp.stack(xdec), jnp.stack(bms))
    cms, e_ac, h_dec = jnp.stack(cms), jnp.stack(e_ac), jnp.stack(h_dec)

    h = state_ref[...]
    for ci in range(nch):
        sl = slice(ci * pairs, (ci + 1) * pairs)
        y = y_diag[sl] + _bmm_nt(cms[sl], h) * e_ac[sl]
        h = h * h_dec[sl] + states[sl]
        for p in range(pairs):
            y_ref[ci * CHUNK:(ci + 1) * CHUNK, p * LANES:(p + 1) * LANES] = y[p]
    state_ref[...] = h

    y = y_ref[...] + dskip_ref[...] * xs
    y = y * _silu(z_ref[...])
    y = y * lax.rsqrt(jnp.mean(y * y, axis=-1, keepdims=True) + RMS_EPS) * normw_ref[...]
    o_ref[...] = y.astype(o_ref.dtype)


def _mamba2_ssd(proj, conv_w, conv_b, a_log, dt_bias, d_skip, norm_w, *, bsz, seq, rows=256,
                regroup_wt=None, regroup_layer=None):
    nt = seq // rows
    w = SSD_WIDTH
    regroup = regroup_wt is not None
    assert not regroup or bsz * nt >= MIXER_BLOCKS
    nexpa = jnp.zeros((1, LANES), F32).at[0, SMALL_DT:SMALL_DT + SSD_HEADS].set(-jnp.exp(a_log))
    dtb = jnp.zeros((1, LANES), F32).at[0, SMALL_DT:SMALL_DT + SSD_HEADS].set(dt_bias)
    d_full = jnp.repeat(d_skip, SSD_HEAD_DIM).reshape(1, w)

    def rowblk(width, col):
        return pl.BlockSpec((rows, width), lambda b, t: (b * nt + t, col // width))

    def const(shape):
        return pl.BlockSpec(shape, lambda b, t: (0,) * len(shape))

    in_specs = [rowblk(w, COL_Z), rowblk(w, COL_XS), rowblk(SSD_BC_WIDTH, COL_BC), rowblk(LANES, COL_SMALL),
                const((SHORT_CONV, w)), const((SHORT_CONV, SSD_BC_WIDTH)), const((1, w)),
                const((1, SSD_BC_WIDTH)), const((1, LANES)), const((1, LANES)), const((1, w)), const((1, w))]
    args = [proj, proj, proj, proj, conv_w[:, :w], conv_w[:, w:], conv_b[:w].reshape(1, w),
            conv_b[w:].reshape(1, SSD_BC_WIDTH), nexpa, dtb, d_full, norm_w.reshape(1, w)]
    out_shape = jax.ShapeDtypeStruct((bsz * seq, w), BF16)
    out_specs = pl.BlockSpec((rows, w), lambda b, t: (b * nt + t, 0))
    if regroup:
        d = regroup_wt.shape[2]
        wt_specs, wsmall_spec = _mixer_weight_specs(d, regroup_layer, lambda b, t: b * nt + t)
        in_specs += wt_specs
        args += [regroup_wt, regroup_wt]
        out_shape = (out_shape, jax.ShapeDtypeStruct((d, PROJ_COLS), BF16))
        out_specs = (out_specs, wsmall_spec)
    return pl.pallas_call(
        functools.partial(_ssd_kernel, rows=rows, regroup=regroup),
        out_shape=out_shape, grid=(bsz, nt), in_specs=in_specs, out_specs=out_specs,
        scratch_shapes=[pltpu.VMEM((8 + rows, w), F32), pltpu.VMEM((8 + rows, SSD_BC_WIDTH), F32),
                        pltpu.VMEM((SSD_HEADS // 2, 2 * SSD_HEAD_DIM, SSD_STATE), F32),
                        pltpu.VMEM((rows, w), F32)],
        compiler_params=_params(("arbitrary", "arbitrary") if regroup else ("parallel", "arbitrary")),
        name="mamba2_ssd")(*args)


PREP_TN = 512
MIXER_BLOCKS = PROJ_COLS // PREP_TN
SMALL_BLOCK = COL_SMALL // PREP_TN


def _mixer_weight_rows(j):
    o = _REF_OFFS
    q0, conf0, kv = (c // PREP_TN for c in (COL_SWA_Q, COL_CONF, COL_SWA_K))
    c = j * PREP_TN
    return jnp.where(j < q0, c, jnp.where(j < conf0, c + (o[4] - COL_SWA_Q),
                     jnp.where(j < kv, c + (o[7] - COL_CONF), jnp.where(j < SMALL_BLOCK, o[5], o[2]))))


def _regroup_mixer_block(j, wt_ref, dt_ref, o_ref):
    @pl.when(j < SMALL_BLOCK)
    def _():
        o_ref[...] = wt_ref[0].T.astype(BF16)

    @pl.when(j == SMALL_BLOCK)
    def _():
        pad = jnp.zeros((PREP_TN - SMALL_DT - SSD_HEADS, wt_ref.shape[2]), F32)
        rows = jnp.concatenate([wt_ref[0, 0:SMALL_DT, :], dt_ref[0], pad], axis=0)
        o_ref[...] = rows.T.astype(BF16)


def _mixer_weight_specs(d, layer, block_of):
    o = _REF_OFFS
    assert o[2] == COL_SWA_Q and o[3] - o[2] == GDN_HEADS == SMALL_B and o[4] - o[2] == SMALL_DT

    def blk(*idx):
        return jnp.minimum(block_of(*idx), MIXER_BLOCKS - 1)

    in_specs = [pl.BlockSpec((pl.Element(1), pl.Element(PREP_TN), pl.Element(d)),
                             lambda *idx: (layer, pl.multiple_of(_mixer_weight_rows(blk(*idx)), 8), 0)),
                pl.BlockSpec((pl.Element(1), pl.Element(SSD_HEADS), pl.Element(d)),
                             lambda *idx: (layer, o[10], 0))]
    out_spec = pl.BlockSpec((d, PREP_TN), lambda *idx: (0, blk(*idx)))
    return in_specs, out_spec


def _regroup_mixer_kernel(wt_ref, dt_ref, o_ref):
    _regroup_mixer_block(pl.program_id(0), wt_ref, dt_ref, o_ref)


def _regroup_mixer_weights(wt, layer):
    d = wt.shape[2]
    in_specs, out_spec = _mixer_weight_specs(d, layer, lambda j: j)
    return pl.pallas_call(
        _regroup_mixer_kernel, out_shape=jax.ShapeDtypeStruct((d, PROJ_COLS), BF16),
        grid=(MIXER_BLOCKS,), in_specs=in_specs, out_specs=out_spec,
        compiler_params=_params(("parallel",)), name="regroup_mixer_weights")(wt, wt)


def _rotary_tables(seq):
    inv_freq = ROPE_THETA ** (-jnp.arange(0, SWA_HEAD_DIM, 2, dtype=F32) / SWA_HEAD_DIM)
    ang = jnp.arange(seq, dtype=F32)[:, None] * inv_freq[None, :]
    cos, sin = jnp.cos(ang), jnp.sin(ang)
    return jnp.concatenate([cos, cos], axis=1), jnp.concatenate([-sin, sin], axis=1)


def kernel(x, w_in, gdn_conv_w, gdn_a_log, gdn_dt_bias, gdn_norm_w, swa_sinks, conf_conv_w, conf_conv_b, conf_ln_g, conf_ln_b, conf_pw_w, conf_pw_b, ssd_conv_w, ssd_conv_b, ssd_a_log, ssd_dt_bias, ssd_d, ssd_norm_w, w_branch, w_out, ln1_g, ln1_b, w_gate_up, w_down, ln2_g, ln2_b):
    bsz, seq, d = x.shape
    m = bsz * seq
    cos_full, sin_signed = _rotary_tables(seq)
    xf = x.reshape(m, d)
    xb = xf.astype(BF16)
    w_in_t = jnp.swapaxes(w_in, 1, 2)
    ssd_rows = 256
    w_small = _regroup_mixer_weights(w_in_t, 0)
    for i in range(DEPTH):
        proj = _matmul(xb, w_small, tm=512, tn=1536, out_dtype=F32, name="in_proj")
        o_a = _gated_deltanet(proj, gdn_conv_w[i], gdn_a_log[i], gdn_dt_bias[i], gdn_norm_w[i],
                              bsz=bsz, seq=seq)
        o_b = _sliding_window_attention(proj, swa_sinks[i], cos_full, sin_signed, bsz=bsz, seq=seq)
        o_c, w_gates = _conformer_conv(proj, conf_conv_w[i], conf_conv_b[i], conf_ln_g[i], conf_ln_b[i],
                                       conf_pw_w[i], conf_pw_b[i], w_in_t, i, bsz=bsz, seq=seq)
        ssd_args = (proj, ssd_conv_w[i], ssd_conv_b[i], ssd_a_log[i], ssd_dt_bias[i], ssd_d[i], ssd_norm_w[i])
        if i + 1 < DEPTH and bsz * (seq // ssd_rows) >= MIXER_BLOCKS:
            o_d, w_small = _mamba2_ssd(*ssd_args, bsz=bsz, seq=seq, rows=ssd_rows,
                                       regroup_wt=w_in_t, regroup_layer=i + 1)
        else:
            o_d = _mamba2_ssd(*ssd_args, bsz=bsz, seq=seq, rows=ssd_rows)
            if i + 1 < DEPTH:
                w_small = _regroup_mixer_weights(w_in_t, i + 1)
        merged, w_out_b = _merge(xb, (o_a, o_b, o_c, o_d), w_gates, w_branch, w_out, i, tm=512, tn=256)
        s1 = _matmul(merged, w_out_b, tm=512, tn=1024, out_dtype=F32, resid=xf, name="out_proj")
        xf, xb = _layer_norm(s1, ln1_g[i], ln1_b[i], tr=256)
        act, w_down_b = _swiglu(xb, w_gate_up, w_down, i, tm=1024, tn=256)
        s2 = _matmul(act, w_down_b, tm=512, tn=512, out_dtype=F32, resid=xf, name="down_proj")
        xf, xb = _layer_norm(s2, ln2_g[i], ln2_b[i], tr=256)
    return xf.reshape(bsz, seq, d)
```

```python
import functools

import jax
import jax.numpy as jnp
import numpy as np
from jax import lax
from jax.experimental import pallas as pl
from jax.experimental.pallas import tpu as pltpu

F32 = jnp.float32
BF16 = jnp.bfloat16

D_MODEL = 4096
DEPTH = 2
N_BRANCH = 4
BRANCH_WIDTH = D_MODEL // 4

GDN_HEAD_DIM = 128
GDN_HEADS = BRANCH_WIDTH // GDN_HEAD_DIM
GDN_WIDTH = GDN_HEADS * GDN_HEAD_DIM
CHUNK = 64
SHORT_CONV = 4

SWA_HEAD_DIM = 128
SWA_Q_HEADS = BRANCH_WIDTH // SWA_HEAD_DIM
SWA_KV_HEADS = SWA_Q_HEADS // 4
SWA_REP = SWA_Q_HEADS // SWA_KV_HEADS
SWA_WINDOW = 128
ROPE_THETA = 10000.0

CONF_WIDTH = BRANCH_WIDTH
CONF_CONV_WIDTH = 31

SSD_HEAD_DIM = 64
SSD_WIDTH = BRANCH_WIDTH
SSD_HEADS = SSD_WIDTH // SSD_HEAD_DIM
SSD_GROUPS = 2
SSD_HEADS_PER_GROUP = SSD_HEADS // SSD_GROUPS
SSD_STATE = 128
SSD_BC_WIDTH = 2 * SSD_GROUPS * SSD_STATE

FFN_HIDDEN = -(-8 * D_MODEL // (3 * 256)) * 256

ALPHA = (2.0 * DEPTH) ** 0.25
LN_EPS = 1e-5
RMS_EPS = 1e-6

LANES = 128
VMEM_LIMIT = 52 * 1024 * 1024

COL_QKV = 0
COL_GATE = COL_QKV + 3 * GDN_WIDTH
COL_SWA_Q = COL_GATE + GDN_WIDTH
COL_CONF = COL_SWA_Q + BRANCH_WIDTH
COL_Z = COL_CONF + 2 * CONF_WIDTH
COL_XS = COL_Z + SSD_WIDTH
COL_BC = COL_XS + SSD_WIDTH
COL_SWA_K = COL_BC + SSD_BC_WIDTH
COL_SWA_V = COL_SWA_K + SWA_KV_HEADS * SWA_HEAD_DIM
COL_SMALL = COL_SWA_V + SWA_KV_HEADS * SWA_HEAD_DIM
SMALL_A = 0
SMALL_B = GDN_HEADS
SMALL_DT = 2 * GDN_HEADS
PROJ_COLS = 10752

_REF_SPLITS = (3 * GDN_WIDTH, GDN_WIDTH, GDN_HEADS, GDN_HEADS, BRANCH_WIDTH, 256, 256,
               2 * CONF_WIDTH, SSD_WIDTH, SSD_WIDTH + SSD_BC_WIDTH, SSD_HEADS, N_BRANCH * D_MODEL)
_REF_OFFS = np.concatenate([[0], np.cumsum(_REF_SPLITS)]).tolist()


def _params(sem, vmem=VMEM_LIMIT):
    return pltpu.CompilerParams(dimension_semantics=sem, vmem_limit_bytes=vmem)


def _bdot(a, b):
    return jnp.dot(a.astype(BF16), b.astype(BF16), preferred_element_type=F32)


def _bdot_nt(a, b):
    return lax.dot_general(a.astype(BF16), b.astype(BF16), (((1,), (1,)), ((), ())),
                           preferred_element_type=F32)


def _silu(x):
    return x * jax.nn.sigmoid(x)


def _softplus(x):
    return jnp.maximum(x, 0.0) + jnp.log(1.0 + jnp.exp(-jnp.abs(x)))


def _cast_weights_once(w_refs, wb_refs):
    @pl.when(pl.program_id(1) == 0)
    def _():
        for w_ref, wb_ref in zip(w_refs, wb_refs):
            wb_ref[...] = w_ref[...].astype(BF16)


def _mm_kernel(x_ref, w_ref, *rest, has_resid):
    r_ref, o_ref = rest if has_resid else (None, rest[0])
    acc = jnp.dot(x_ref[...], w_ref[...], preferred_element_type=F32)
    if has_resid:
        acc = ALPHA * r_ref[...] + acc
    o_ref[...] = acc.astype(o_ref.dtype)


def _matmul(x, w, *, tm, tn, out_dtype, resid=None, name):
    m, k = x.shape
    n = w.shape[1]
    assert x.dtype == BF16 and w.dtype == BF16
    in_specs = [pl.BlockSpec((tm, k), lambda j, i: (i, 0)),
                pl.BlockSpec((k, tn), lambda j, i: (0, j))]
    args = [x, w]
    if resid is not None:
        in_specs.append(pl.BlockSpec((tm, tn), lambda j, i: (i, j)))
        args.append(resid)
    return pl.pallas_call(
        functools.partial(_mm_kernel, has_resid=resid is not None),
        out_shape=jax.ShapeDtypeStruct((m, n), out_dtype),
        grid=(n // tn, m // tm), in_specs=in_specs,
        out_specs=pl.BlockSpec((tm, tn), lambda j, i: (i, j)),
        compiler_params=_params(("parallel", "parallel")), name=name)(*args)


def _swiglu_kernel(x_ref, wg0_ref, wg1_ref, wu0_ref, wu1_ref, wd_ref, o_ref, wdb_ref, wgb_ref, wub_ref):
    half = wg0_ref.shape[0]
    _cast_weights_once([wg0_ref, wg1_ref, wu0_ref, wu1_ref],
                       [wgb_ref.at[0:half], wgb_ref.at[half:2 * half], wub_ref.at[0:half], wub_ref.at[half:2 * half]])
    x = x_ref[...]
    g = jnp.dot(x, wgb_ref[...], preferred_element_type=F32)
    u = jnp.dot(x, wub_ref[...], preferred_element_type=F32)
    o_ref[...] = (_silu(g) * u).astype(o_ref.dtype)
    wdb_ref[...] = wd_ref[...].astype(BF16)


def _swiglu(x, w_gate_up, w_down, layer, *, tm, tn):
    m, k = x.shape
    hidden = w_gate_up.shape[2] // 2
    nb = hidden // tn
    n_inner = m // tm
    d_out = w_down.shape[2]
    wd_rows = hidden // (nb * n_inner)
    assert wd_rows * nb * n_inner == hidden and wd_rows % 16 == 0

    def weight_half(which, up, order):
        def index(j, i):
            ahead = jnp.minimum(j + jnp.where(i > order, 1, 0), nb - 1)
            return (layer, which, ahead + (nb if up else 0))
        return pl.BlockSpec((None, k // 2, tn), index)

    return pl.pallas_call(
        _swiglu_kernel,
        out_shape=(jax.ShapeDtypeStruct((m, hidden), BF16), jax.ShapeDtypeStruct((hidden, d_out), BF16)),
        grid=(nb, n_inner),
        in_specs=[pl.BlockSpec((tm, k), lambda j, i: (i, 0)),
                  weight_half(0, False, 0), weight_half(1, False, 1),
                  weight_half(0, True, 2), weight_half(1, True, 3),
                  pl.BlockSpec((None, wd_rows, d_out), lambda j, i: (layer, j * n_inner + i, 0))],
        out_specs=(pl.BlockSpec((tm, tn), lambda j, i: (i, j)),
                   pl.BlockSpec((wd_rows, d_out), lambda j, i: (j * n_inner + i, 0))),
        scratch_shapes=[pltpu.VMEM((k, tn), BF16), pltpu.VMEM((k, tn), BF16)],
        compiler_params=_params(("arbitrary", "arbitrary")), name="swiglu",
    )(x, *([w_gate_up] * 4), w_down)


def _merge_kernel(x_ref, oa_ref, ob_ref, oc_ref, od_ref,
                  wg0, wg1, wg2, wg3, wb0, wb1, wb2, wb3, wo_ref, o_ref, wob_ref, wbb0, wbb1, wbb2, wbb3):
    _cast_weights_once([wb0, wb1, wb2, wb3], [wbb0, wbb1, wbb2, wbb3])
    wob_ref[...] = wo_ref[...].astype(BF16)
    x = x_ref[...]
    acc = None
    for o_r, wg, wb in ((oa_ref, wg0, wbb0), (ob_ref, wg1, wbb1), (oc_ref, wg2, wbb2), (od_ref, wg3, wbb3)):
        gate = jax.nn.sigmoid(jnp.dot(x, wg[...], preferred_element_type=F32))
        up = jnp.dot(o_r[...], wb[...], preferred_element_type=F32)
        acc = gate * up if acc is None else acc + gate * up
    o_ref[...] = acc.astype(o_ref.dtype)


def _merge(x, branches, w_gates, w_branch, w_out, layer, *, tm, tn):
    m, k = x.shape
    d = w_branch.shape[3]
    nb = d // tn
    bw = w_branch.shape[2]
    n_inner = m // tm
    wo_rows = w_out.shape[1] // (nb * n_inner)
    assert wo_rows * nb * n_inner == w_out.shape[1] and wo_rows % 16 == 0
    in_specs = [pl.BlockSpec((tm, k), lambda j, i: (i, 0))]
    in_specs += [pl.BlockSpec((tm, bw), lambda j, i: (i, 0)) for _ in range(N_BRANCH)]
    in_specs += [pl.BlockSpec((k, tn), functools.partial(lambda j, i, g: (0, g * nb + j), g=g))
                 for g in range(N_BRANCH)]
    in_specs += [pl.BlockSpec((None, None, bw, tn), functools.partial(lambda j, i, g: (layer, g, 0, j), g=g))
                 for g in range(N_BRANCH)]
    in_specs.append(pl.BlockSpec((None, wo_rows, w_out.shape[2]), lambda j, i: (layer, j * n_inner + i, 0)))
    return pl.pallas_call(
        _merge_kernel,
        out_shape=(jax.ShapeDtypeStruct((m, d), BF16), jax.ShapeDtypeStruct(w_out.shape[1:], BF16)),
        grid=(nb, n_inner), in_specs=in_specs,
        out_specs=(pl.BlockSpec((tm, tn), lambda j, i: (i, j)),
                   pl.BlockSpec((wo_rows, w_out.shape[2]), lambda j, i: (j * n_inner + i, 0))),
        scratch_shapes=[pltpu.VMEM((bw, tn), BF16) for _ in range(N_BRANCH)],
        compiler_params=_params(("parallel", "arbitrary")), name="merge",
    )(x, *branches, *([w_gates] * N_BRANCH), *([w_branch] * N_BRANCH), w_out)


def _ln_kernel(s_ref, g_ref, b_ref, of_ref, ob_ref):
    s = s_ref[...]
    mu = jnp.mean(s, axis=-1, keepdims=True)
    c = s - mu
    var = jnp.mean(c * c, axis=-1, keepdims=True)
    y = c * lax.rsqrt(var + LN_EPS) * g_ref[...] + b_ref[...]
    of_ref[...] = y
    ob_ref[...] = y.astype(BF16)


def _layer_norm(s, g, b, *, tr):
    m, d = s.shape
    row = pl.BlockSpec((tr, d), lambda i: (i, 0))
    vec = pl.BlockSpec((1, d), lambda i: (0, 0))
    return pl.pallas_call(
        _ln_kernel,
        out_shape=(jax.ShapeDtypeStruct((m, d), F32), jax.ShapeDtypeStruct((m, d), BF16)),
        grid=(m // tr,), in_specs=[row, vec, vec], out_specs=(row, row),
        compiler_params=_params(("parallel",)), name="layer_norm")(s, g.reshape(1, d), b.reshape(1, d))


def _causal_conv(pad_ref, x, w_ref, halo, t_first, shift_ref=None):
    rows = x.shape[0]
    k = w_ref.shape[0]

    @pl.when(t_first)
    def _():
        pad_ref[0:halo, :] = jnp.zeros((halo, x.shape[1]), F32)

    pad_ref[halo:halo + rows, :] = x
    acc = None
    sublanes = 8
    first = halo - (k - 1)
    for phase in range(min(sublanes, k)):
        taps = list(range(phase, k, sublanes))
        lo = first + taps[0]
        if shift_ref is not None:
            span = rows + taps[-1] - taps[0]
            shift_ref[0:span, :] = pad_ref[lo:lo + span, :]
        for j in taps:
            if shift_ref is not None:
                window = shift_ref[j - taps[0]:j - taps[0] + rows, :]
            else:
                window = pad_ref[first + j:first + j + rows, :]
            term = w_ref[j:j + 1, :] * window
            acc = term if acc is None else acc + term
    pad_ref[0:halo, :] = x[rows - halo:rows, :]
    return acc


def _chunk_cumsum(x):
    r = lax.broadcasted_iota(jnp.int32, (CHUNK, CHUNK), 0)
    c = lax.broadcasted_iota(jnp.int32, (CHUNK, CHUNK), 1)
    tril = jnp.where(r >= c, 1.0, 0.0).astype(F32)
    return jnp.dot(tril, x, preferred_element_type=F32, precision=lax.Precision.HIGHEST)


def _bmm(a, b):
    return lax.dot_general(a.astype(BF16), b.astype(BF16), (((2,), (1,)), ((0,), (0,))),
                           preferred_element_type=F32)


def _bmm_nt(a, b):
    return lax.dot_general(a.astype(BF16), b.astype(BF16), (((2,), (2,)), ((0,), (0,))),
                           preferred_element_type=F32)


def _bmm_tn(a, b):
    return lax.dot_general(a.astype(BF16), b.astype(BF16), (((1,), (1,)), ((0,), (0,))),
                           preferred_element_type=F32)


def _unit_lower_inverse(a):
    r = lax.broadcasted_iota(jnp.int32, (1, CHUNK, CHUNK), 1)
    c = lax.broadcasted_iota(jnp.int32, (1, CHUNK, CHUNK), 2)
    same16 = (r >> 4) == (c >> 4)
    same32 = (r >> 5) == (c >> 5)
    eye = jnp.where(r == c, 1.0, 0.0).astype(F32)
    ad = jnp.where(same16, a, 0.0)
    o1 = jnp.where(same32, a, 0.0) - ad
    o2 = jnp.where(same32, 0.0, a)
    a2 = _bmm(ad, ad)
    d = eye - ad
    a4 = _bmm(a2, a2)
    d = d + _bmm(d, a2)
    a8 = _bmm(a4, a4)
    d = d + _bmm(d, a4)
    d = d + _bmm(d, a8)
    d = d - _bmm(d, _bmm(o1, d))
    d = d - _bmm(d, _bmm(o2, d))
    return d


def _causal_masks(lead=()):
    shape = lead + (CHUNK, CHUNK)
    r = lax.broadcasted_iota(jnp.int32, shape, len(lead))
    c = lax.broadcasted_iota(jnp.int32, shape, len(lead) + 1)
    return r >= c, r > c


def _decay_matrix(col, row, causal):
    return jnp.where(causal, jnp.exp(jnp.where(causal, col - row, 0.0)), 0.0)


def _gdn_kernel(qkv_ref, gate_ref, small_ref, convw_ref, nexpa_ref, dtb_ref, normw_ref,
                o_ref, pad_ref, state_ref, *, rows):
    t_first = pl.program_id(1) == 0

    @pl.when(t_first)
    def _():
        state_ref[...] = jnp.zeros(state_ref.shape, F32)

    y = _silu(_causal_conv(pad_ref, qkv_ref[...], convw_ref, 8, t_first))
    small = small_ref[...]
    g_all = nexpa_ref[...] * _softplus(small + dtb_ref[...])
    beta_all = jax.nn.sigmoid(small)
    causal, strict = _causal_masks((1,))
    nch = rows // CHUNK
    gcums = [_chunk_cumsum(g_all[ci * CHUNK:(ci + 1) * CHUNK, :]) for ci in range(nch)]
    gcums_t = [g.T for g in gcums]

    def stack(fn):
        return jnp.stack([fn(ci, h) for ci in range(nch) for h in range(GDN_HEADS)], axis=0)

    def head_cols(base):
        return stack(lambda ci, h: y[ci * CHUNK:(ci + 1) * CHUNK,
                                     base + h * GDN_HEAD_DIM:base + (h + 1) * GDN_HEAD_DIM])

    q, k, v = head_cols(0), head_cols(GDN_WIDTH), head_cols(2 * GDN_WIDTH)
    q = q * lax.rsqrt(jnp.sum(q * q, axis=-1, keepdims=True) + RMS_EPS) * (GDN_HEAD_DIM ** -0.5)
    k = k * lax.rsqrt(jnp.sum(k * k, axis=-1, keepdims=True) + RMS_EPS)
    beta = stack(lambda ci, h: beta_all[ci * CHUNK:(ci + 1) * CHUNK, SMALL_B + h:SMALL_B + h + 1])
    gc = stack(lambda ci, h: gcums[ci][:, SMALL_A + h:SMALL_A + h + 1])
    gr = stack(lambda ci, h: gcums_t[ci][SMALL_A + h:SMALL_A + h + 1, :])
    g_last = gc[:, CHUNK - 1:CHUNK, :]
    decay = _decay_matrix(gc, gr, causal)
    kb = k * beta
    t_mat = _unit_lower_inverse(jnp.where(strict, _bmm_nt(kb, k) * decay, 0.0))
    e_gc = jnp.exp(gc)
    u = _bmm(t_mat, v * beta)
    w = _bmm(t_mat, kb * e_gc)
    qk = _bmm_nt(q, k) * decay
    q_dec = q * e_gc
    k_dec = k * jnp.exp(g_last - gc)
    chunk_decay = jnp.exp(g_last)

    norm_w = normw_ref[...]
    state = state_ref[...]
    for ci in range(nch):
        sl = slice(ci * GDN_HEADS, (ci + 1) * GDN_HEADS)
        v_new = u[sl] - _bmm(w[sl], state)
        o = _bmm(q_dec[sl], state) + _bmm(qk[sl], v_new)
        state = state * chunk_decay[sl] + _bmm_tn(k_dec[sl], v_new)
        o = o * lax.rsqrt(jnp.mean(o * o, axis=-1, keepdims=True) + RMS_EPS) * norm_w
        for h in range(GDN_HEADS):
            c0 = h * GDN_HEAD_DIM
            gate = gate_ref[ci * CHUNK:(ci + 1) * CHUNK, c0:c0 + GDN_HEAD_DIM]
            o_ref[ci * CHUNK:(ci + 1) * CHUNK, c0:c0 + GDN_HEAD_DIM] = (o[h] * _silu(gate)).astype(o_ref.dtype)
    state_ref[...] = state


def _gated_deltanet(proj, conv_w, a_log, dt_bias, norm_w, *, bsz, seq, rows=256):
    nt = seq // rows
    nexpa = jnp.zeros((1, LANES), F32).at[0, SMALL_A:SMALL_A + GDN_HEADS].set(-jnp.exp(a_log))
    dtb = jnp.zeros((1, LANES), F32).at[0, SMALL_A:SMALL_A + GDN_HEADS].set(dt_bias)

    def rowblk(width, col):
        return pl.BlockSpec((rows, width), lambda b, t: (b * nt + t, col // width))

    def const(shape):
        return pl.BlockSpec(shape, lambda b, t: (0,) * len(shape))

    return pl.pallas_call(
        functools.partial(_gdn_kernel, rows=rows),
        out_shape=jax.ShapeDtypeStruct((bsz * seq, GDN_WIDTH), BF16),
        grid=(bsz, nt),
        in_specs=[rowblk(3 * GDN_WIDTH, COL_QKV), rowblk(GDN_WIDTH, COL_GATE), rowblk(LANES, COL_SMALL),
                  const((SHORT_CONV, 3 * GDN_WIDTH)), const((1, LANES)), const((1, LANES)),
                  const((1, GDN_HEAD_DIM))],
        out_specs=pl.BlockSpec((rows, GDN_WIDTH), lambda b, t: (b * nt + t, 0)),
        scratch_shapes=[pltpu.VMEM((8 + rows, 3 * GDN_WIDTH), F32),
                        pltpu.VMEM((GDN_HEADS, GDN_HEAD_DIM, GDN_HEAD_DIM), F32)],
        compiler_params=_params(("parallel", "arbitrary")), name="gated_deltanet",
    )(proj, proj, proj, conv_w, nexpa, dtb, norm_w.reshape(1, GDN_HEAD_DIM))


def _swa_kernel(sinks_ref, q_ref, kc_ref, kp_ref, vc_ref, vp_ref, cos_ref, sin_ref, cosp_ref, sinp_ref,
                o_ref):
    n = pl.program_id(1)
    blk = SWA_WINDOW
    cos, sin = cos_ref[...], sin_ref[...]
    cosp, sinp = cosp_ref[...], sinp_ref[...]

    def rope(x, c, s):
        return x * c + pltpu.roll(x, SWA_HEAD_DIM // 2, axis=1) * s

    rows = SWA_REP * blk
    qi = lax.broadcasted_iota(jnp.int32, (rows, 2 * blk), 0) & (blk - 1)
    kj = lax.broadcasted_iota(jnp.int32, (rows, 2 * blk), 1)
    dist = qi + blk - kj
    in_window = jnp.where(dist >= 0, jnp.where(dist < SWA_WINDOW, 1, 0), 0)
    has_prev = jnp.where(kj >= blk, 1, jnp.where(n > 0, 1, 0))
    mask = (in_window * has_prev) > 0
    head_of_row = lax.broadcasted_iota(jnp.int32, (rows, 1), 0) >> 7

    for g in range(SWA_KV_HEADS):
        d0 = g * SWA_HEAD_DIM
        q_rows = [rope(q_ref[:, (g * SWA_REP + j) * SWA_HEAD_DIM:(g * SWA_REP + j + 1) * SWA_HEAD_DIM], cos, sin)
                  for j in range(SWA_REP)]
        qs = jnp.concatenate(q_rows, axis=0)
        kband = jnp.concatenate([rope(kp_ref[:, d0:d0 + SWA_HEAD_DIM], cosp, sinp),
                                 rope(kc_ref[:, d0:d0 + SWA_HEAD_DIM], cos, sin)], axis=0)
        vband = jnp.concatenate([vp_ref[:, d0:d0 + SWA_HEAD_DIM], vc_ref[:, d0:d0 + SWA_HEAD_DIM]], axis=0)
        s = _bdot_nt(qs, kband) * (SWA_HEAD_DIM ** -0.5)
        s = jnp.where(mask, s, -jnp.inf)
        sink = jnp.zeros((rows, 1), F32)
        for j in range(SWA_REP):
            sink = jnp.where(head_of_row == j, sinks_ref[g * SWA_REP + j], sink)
        m = jnp.maximum(jnp.max(s, axis=-1, keepdims=True), sink)
        p = jnp.exp(s - m)
        denom = jnp.sum(p, axis=-1, keepdims=True) + jnp.exp(sink - m)
        o = _bdot(p / denom, vband)
        for j in range(SWA_REP):
            c0 = (g * SWA_REP + j) * SWA_HEAD_DIM
            o_ref[:, c0:c0 + SWA_HEAD_DIM] = o[j * blk:(j + 1) * blk, :].astype(o_ref.dtype)


def _sliding_window_attention(proj, sinks, cos_full, sin_signed, *, bsz, seq):
    blk = SWA_WINDOW
    nb = seq // blk
    kvw = SWA_KV_HEADS * SWA_HEAD_DIM

    def cur(width, col):
        return pl.BlockSpec((blk, width), lambda b, n: (b * nb + n, col // width))

    def prev(width, col):
        return pl.BlockSpec((blk, width), lambda b, n: (b * nb + jnp.maximum(n - 1, 0), col // width))

    tab_cur = pl.BlockSpec((blk, SWA_HEAD_DIM), lambda b, n: (n, 0))
    tab_prev = pl.BlockSpec((blk, SWA_HEAD_DIM), lambda b, n: (jnp.maximum(n - 1, 0), 0))
    return pl.pallas_call(
        _swa_kernel, out_shape=jax.ShapeDtypeStruct((bsz * seq, BRANCH_WIDTH), BF16),
        grid=(bsz, nb),
        in_specs=[pl.BlockSpec(memory_space=pltpu.SMEM),
                  cur(BRANCH_WIDTH, COL_SWA_Q), cur(kvw, COL_SWA_K), prev(kvw, COL_SWA_K),
                  cur(kvw, COL_SWA_V), prev(kvw, COL_SWA_V), tab_cur, tab_cur, tab_prev, tab_prev],
        out_specs=pl.BlockSpec((blk, BRANCH_WIDTH), lambda b, n: (b * nb + n, 0)),
        compiler_params=_params(("parallel", "parallel")), name="sliding_window_attention",
    )(sinks, proj, proj, proj, proj, proj, cos_full, sin_signed, cos_full, sin_signed)


def _conf_kernel(a_ref, gt_ref, convw_ref, convb_ref, lng_ref, lnb_ref, pww_ref, pwb_ref, wt_ref,
                 o_ref, wg_ref, pad_ref, shift_ref):
    wg_ref[...] = wt_ref[0].T.astype(BF16)
    t_first = pl.program_id(1) == 0
    h = a_ref[...] * jax.nn.sigmoid(gt_ref[...])
    h = _causal_conv(pad_ref, h, convw_ref, 32, t_first, shift_ref) + convb_ref[...]
    mu = jnp.mean(h, axis=-1, keepdims=True)
    c = h - mu
    var = jnp.mean(c * c, axis=-1, keepdims=True)
    h = _silu(c * lax.rsqrt(var + LN_EPS) * lng_ref[...] + lnb_ref[...])
    o_ref[...] = (jnp.dot(h.astype(BF16), pww_ref[...], preferred_element_type=F32)
                  + pwb_ref[...]).astype(o_ref.dtype)


def _conformer_conv(proj, conv_w, conv_b, ln_g, ln_b, pw_w, pw_b, wt, layer, *, bsz, seq, rows=256):
    nt = seq // rows
    w = CONF_WIDTH
    d = wt.shape[2]
    gate_cols = N_BRANCH * D_MODEL
    gate_rows = gate_cols // (bsz * nt)
    assert gate_rows * bsz * nt == gate_cols and gate_rows % LANES == 0
    gate0 = _REF_OFFS[11]

    def rowblk(col):
        return pl.BlockSpec((rows, w), lambda b, t: (b * nt + t, col // w))

    def const(shape):
        return pl.BlockSpec(shape, lambda b, t: (0,) * len(shape))

    wt_spec = pl.BlockSpec((pl.Element(1), pl.Element(gate_rows), pl.Element(d)),
                           lambda b, t: (layer, pl.multiple_of(gate0 + (b * nt + t) * gate_rows, 8), 0))
    return pl.pallas_call(
        _conf_kernel,
        out_shape=(jax.ShapeDtypeStruct((bsz * seq, w), BF16), jax.ShapeDtypeStruct((d, gate_cols), BF16)),
        grid=(bsz, nt),
        in_specs=[rowblk(COL_CONF), rowblk(COL_CONF + w), const((CONF_CONV_WIDTH, w)), const((1, w)),
                  const((1, w)), const((1, w)), const((w, w)), const((1, w)), wt_spec],
        out_specs=(pl.BlockSpec((rows, w), lambda b, t: (b * nt + t, 0)),
                   pl.BlockSpec((d, gate_rows), lambda b, t: (0, b * nt + t))),
        scratch_shapes=[pltpu.VMEM((32 + rows, w), F32),
                        pltpu.VMEM((rows + 8 * ((CONF_CONV_WIDTH - 1) // 8), w), F32)],
        compiler_params=_params(("parallel", "arbitrary")), name="conformer_conv",
    )(proj, proj, conv_w, conv_b.reshape(1, w), ln_g.reshape(1, w), ln_b.reshape(1, w),
      pw_w.astype(BF16), pw_b.reshape(1, w), wt)


def _ssd_kernel(z_ref, xs_ref, bc_ref, small_ref, wx_ref, wbc_ref, bx_ref, bbc_ref, nexpa_ref, dtb_ref,
                dskip_ref, normw_ref, *rest, rows, regroup):
    if regroup:
        wt_ref, wdt_ref, o_ref, wsmall_ref, xpad_ref, bcpad_ref, state_ref, y_ref = rest
        step = pl.program_id(0) * pl.num_programs(1) + pl.program_id(1)
        _regroup_mixer_block(step, wt_ref, wdt_ref, wsmall_ref)
    else:
        o_ref, xpad_ref, bcpad_ref, state_ref, y_ref = rest
    t_first = pl.program_id(1) == 0

    @pl.when(t_first)
    def _():
        state_ref[...] = jnp.zeros(state_ref.shape, F32)

    xs = _silu(_causal_conv(xpad_ref, xs_ref[...], wx_ref, 8, t_first) + bx_ref[...])
    bc = _silu(_causal_conv(bcpad_ref, bc_ref[...], wbc_ref, 8, t_first) + bbc_ref[...])
    dt_all = _softplus(small_ref[...] + dtb_ref[...])
    a_all = nexpa_ref[...] * dt_all
    gs = SSD_GROUPS * SSD_STATE
    nch = rows // CHUNK
    pairs = SSD_HEADS // 2
    pairs_per_group = pairs // SSD_GROUPS

    er = lax.broadcasted_iota(jnp.int32, (LANES, SSD_WIDTH), 0)
    ec = lax.broadcasted_iota(jnp.int32, (LANES, SSD_WIDTH), 1)
    expand = jnp.where(er == SMALL_DT + (ec >> 6), 1.0, 0.0).astype(F32)

    def per_head_cols(v):
        return jnp.dot(v, expand, preferred_element_type=F32, precision=lax.Precision.HIGHEST)

    xdt = xs * per_head_cols(dt_all)
    lane = lax.broadcasted_iota(jnp.int32, (1, LANES), 1)
    low = lane < SSD_HEAD_DIM
    row2 = lax.broadcasted_iota(jnp.int32, (2 * CHUNK, 1), 0)
    l_idx = lax.broadcasted_iota(jnp.int32, (CHUNK, LANES), 0)
    s_idx = lax.broadcasted_iota(jnp.int32, (CHUNK, LANES), 1) & (CHUNK - 1)
    causal2 = l_idx >= s_idx

    lhs, rhs, xdec, bms, cms, e_ac, h_dec = [], [], [], [], [], [], []
    for ci in range(nch):
        r0 = ci * CHUNK
        acs = _chunk_cumsum(a_all[r0:r0 + CHUNK, :])
        ac_cols = per_head_cols(acs)
        acs_t = jnp.concatenate([acs, acs], axis=0).T
        a_last_cols = ac_cols[CHUNK - 1:CHUNK, :]
        xdec_c = xdt[r0:r0 + CHUNK, :] * jnp.exp(a_last_cols - ac_cols)
        e_ac_c = jnp.exp(ac_cols)
        cb2 = []
        for g in range(SSD_GROUPS):
            bm = bc[r0:r0 + CHUNK, g * SSD_STATE:(g + 1) * SSD_STATE]
            cm = bc[r0:r0 + CHUNK, gs + g * SSD_STATE:gs + (g + 1) * SSD_STATE]
            cb2.append(_bdot_nt(cm, jnp.concatenate([bm, bm], axis=0)))
        for p in range(pairs):
            g = p // pairs_per_group
            c0 = p * LANES
            l0 = SMALL_DT + 2 * p
            a_row = jnp.where(low, acs_t[l0:l0 + 1, :], acs_t[l0 + 1:l0 + 2, :])
            seg = _decay_matrix(ac_cols[:, c0:c0 + LANES], a_row, causal2)
            lhs.append(cb2[g] * seg)
            xp = xdt[r0:r0 + CHUNK, c0:c0 + LANES]
            rhs.append(jnp.concatenate([jnp.where(low, xp, 0.0), jnp.where(low, 0.0, xp)], axis=0))
            xdec.append(xdec_c[:, c0:c0 + LANES])
            bms.append(bc[r0:r0 + CHUNK, g * SSD_STATE:(g + 1) * SSD_STATE])
            cms.append(bc[r0:r0 + CHUNK, gs + g * SSD_STATE:gs + (g + 1) * SSD_STATE])
            e_ac.append(e_ac_c[:, c0:c0 + LANES])
            a_last = jnp.where(row2 < SSD_HEAD_DIM, acs[CHUNK - 1:CHUNK, l0:l0 + 1],
                               acs[CHUNK - 1:CHUNK, l0 + 1:l0 + 2])
            h_dec.append(jnp.exp(a_last))

    y_diag = _bmm(jnp.stack(lhs), jnp.stack(rhs))
    states = _bmm_tn(jnp.stack(xdec), jnp.stack(bms))
    cms, e_ac, h_dec = jnp.stack(cms), jnp.stack(e_ac), jnp.stack(h_dec)

    h = state_ref[...]
    for ci in range(nch):
        sl = slice(ci * pairs, (ci + 1) * pairs)
        y = y_diag[sl] + _bmm_nt(cms[sl], h) * e_ac[sl]
        h = h * h_dec[sl] + states[sl]
        for p in range(pairs):
            y_ref[ci * CHUNK:(ci + 1) * CHUNK, p * LANES:(p + 1) * LANES] = y[p]
    state_ref[...] = h

    y = y_ref[...] + dskip_ref[...] * xs
    y = y * _silu(z_ref[...])
    y = y * lax.rsqrt(jnp.mean(y * y, axis=-1, keepdims=True) + RMS_EPS) * normw_ref[...]
    o_ref[...] = y.astype(o_ref.dtype)


def _mamba2_ssd(proj, conv_w, conv_b, a_log, dt_bias, d_skip, norm_w, *, bsz, seq, rows=256,
                regroup_wt=None, regroup_layer=None):
    nt = seq // rows
    w = SSD_WIDTH
    regroup = regroup_wt is not None
    assert not regroup or bsz * nt >= MIXER_BLOCKS
    nexpa = jnp.zeros((1, LANES), F32).at[0, SMALL_DT:SMALL_DT + SSD_HEADS].set(-jnp.exp(a_log))
    dtb = jnp.zeros((1, LANES), F32).at[0, SMALL_DT:SMALL_DT + SSD_HEADS].set(dt_bias)
    d_full = jnp.repeat(d_skip, SSD_HEAD_DIM).reshape(1, w)

    def rowblk(width, col):
        return pl.BlockSpec((rows, width), lambda b, t: (b * nt + t, col // width))

    def const(shape):
        return pl.BlockSpec(shape, lambda b, t: (0,) * len(shape))

    in_specs = [rowblk(w, COL_Z), rowblk(w, COL_XS), rowblk(SSD_BC_WIDTH, COL_BC), rowblk(LANES, COL_SMALL),
                const((SHORT_CONV, w)), const((SHORT_CONV, SSD_BC_WIDTH)), const((1, w)),
                const((1, SSD_BC_WIDTH)), const((1, LANES)), const((1, LANES)), const((1, w)), const((1, w))]
    args = [proj, proj, proj, proj, conv_w[:, :w], conv_w[:, w:], conv_b[:w].reshape(1, w),
            conv_b[w:].reshape(1, SSD_BC_WIDTH), nexpa, dtb, d_full, norm_w.reshape(1, w)]
    out_shape = jax.ShapeDtypeStruct((bsz * seq, w), BF16)
    out_specs = pl.BlockSpec((rows, w), lambda b, t: (b * nt + t, 0))
    if regroup:
        d = regroup_wt.shape[2]
        wt_specs, wsmall_spec = _mixer_weight_specs(d, regroup_layer, lambda b, t: b * nt + t)
        in_specs += wt_specs
        args += [regroup_wt, regroup_wt]
        out_shape = (out_shape, jax.ShapeDtypeStruct((d, PROJ_COLS), BF16))
        out_specs = (out_specs, wsmall_spec)
    return pl.pallas_call(
        functools.partial(_ssd_kernel, rows=rows, regroup=regroup),
        out_shape=out_shape, grid=(bsz, nt), in_specs=in_specs, out_specs=out_specs,
        scratch_shapes=[pltpu.VMEM((8 + rows, w), F32), pltpu.VMEM((8 + rows, SSD_BC_WIDTH), F32),
                        pltpu.VMEM((SSD_HEADS // 2, 2 * SSD_HEAD_DIM, SSD_STATE), F32),
                        pltpu.VMEM((rows, w), F32)],
        compiler_params=_params(("arbitrary", "arbitrary") if regroup else ("parallel", "arbitrary")),
        name="mamba2_ssd")(*args)


PREP_TN = 512
MIXER_BLOCKS = PROJ_COLS // PREP_TN
SMALL_BLOCK = COL_SMALL // PREP_TN


def _mixer_weight_rows(j):
    o = _REF_OFFS
    q0, conf0, kv = (c // PREP_TN for c in (COL_SWA_Q, COL_CONF, COL_SWA_K))
    c = j * PREP_TN
    return jnp.where(j < q0, c, jnp.where(j < conf0, c + (o[4] - COL_SWA_Q),
                     jnp.where(j < kv, c + (o[7] - COL_CONF), jnp.where(j < SMALL_BLOCK, o[5], o[2]))))


def _regroup_mixer_block(j, wt_ref, dt_ref, o_ref):
    @pl.when(j < SMALL_BLOCK)
    def _():
        o_ref[...] = wt_ref[0].T.astype(BF16)

    @pl.when(j == SMALL_BLOCK)
    def _():
        pad = jnp.zeros((PREP_TN - SMALL_DT - SSD_HEADS, wt_ref.shape[2]), F32)
        rows = jnp.concatenate([wt_ref[0, 0:SMALL_DT, :], dt_ref[0], pad], axis=0)
        o_ref[...] = rows.T.astype(BF16)


def _mixer_weight_specs(d, layer, block_of):
    o = _REF_OFFS
    assert o[2] == COL_SWA_Q and o[3] - o[2] == GDN_HEADS == SMALL_B and o[4] - o[2] == SMALL_DT

    def blk(*idx):
        return jnp.minimum(block_of(*idx), MIXER_BLOCKS - 1)

    in_specs = [pl.BlockSpec((pl.Element(1), pl.Element(PREP_TN), pl.Element(d)),
                             lambda *idx: (layer, pl.multiple_of(_mixer_weight_rows(blk(*idx)), 8), 0)),
                pl.BlockSpec((pl.Element(1), pl.Element(SSD_HEADS), pl.Element(d)),
                             lambda *idx: (layer, o[10], 0))]
    out_spec = pl.BlockSpec((d, PREP_TN), lambda *idx: (0, blk(*idx)))
    return in_specs, out_spec


def _regroup_mixer_kernel(wt_ref, dt_ref, o_ref):
    _regroup_mixer_block(pl.program_id(0), wt_ref, dt_ref, o_ref)


def _regroup_mixer_weights(wt, layer):
    d = wt.shape[2]
    in_specs, out_spec = _mixer_weight_specs(d, layer, lambda j: j)
    return pl.pallas_call(
        _regroup_mixer_kernel, out_shape=jax.ShapeDtypeStruct((d, PROJ_COLS), BF16),
        grid=(MIXER_BLOCKS,), in_specs=in_specs, out_specs=out_spec,
        compiler_params=_params(("parallel",)), name="regroup_mixer_weights")(wt, wt)


def _rotary_tables(seq):
    inv_freq = ROPE_THETA ** (-jnp.arange(0, SWA_HEAD_DIM, 2, dtype=F32) / SWA_HEAD_DIM)
    ang = jnp.arange(seq, dtype=F32)[:, None] * inv_freq[None, :]
    cos, sin = jnp.cos(ang), jnp.sin(ang)
    return jnp.concatenate([cos, cos], axis=1), jnp.concatenate([-sin, sin], axis=1)


def kernel(x, w_in, gdn_conv_w, gdn_a_log, gdn_dt_bias, gdn_norm_w, swa_sinks, conf_conv_w, conf_conv_b, conf_ln_g, conf_ln_b, conf_pw_w, conf_pw_b, ssd_conv_w, ssd_conv_b, ssd_a_log, ssd_dt_bias, ssd_d, ssd_norm_w, w_branch, w_out, ln1_g, ln1_b, w_gate_up, w_down, ln2_g, ln2_b):
    bsz, seq, d = x.shape
    m = bsz * seq
    cos_full, sin_signed = _rotary_tables(seq)
    xf = x.reshape(m, d)
    xb = xf.astype(BF16)
    w_in_t = jnp.swapaxes(w_in, 1, 2)
    ssd_rows = 256
    w_small = _regroup_mixer_weights(w_in_t, 0)
    for i in range(DEPTH):
        proj = _matmul(xb, w_small, tm=512, tn=1536, out_dtype=F32, name="in_proj")
        o_a = _gated_deltanet(proj, gdn_conv_w[i], gdn_a_log[i], gdn_dt_bias[i], gdn_norm_w[i],
                              bsz=bsz, seq=seq)
        o_b = _sliding_window_attention(proj, swa_sinks[i], cos_full, sin_signed, bsz=bsz, seq=seq)
        o_c, w_gates = _conformer_conv(proj, conf_conv_w[i], conf_conv_b[i], conf_ln_g[i], conf_ln_b[i],
                                       conf_pw_w[i], conf_pw_b[i], w_in_t, i, bsz=bsz, seq=seq)
        ssd_args = (proj, ssd_conv_w[i], ssd_conv_b[i], ssd_a_log[i], ssd_dt_bias[i], ssd_d[i], ssd_norm_w[i])
        if i + 1 < DEPTH and bsz * (seq // ssd_rows) >= MIXER_BLOCKS:
            o_d, w_small = _mamba2_ssd(*ssd_args, bsz=bsz, seq=seq, rows=ssd_rows,
                                       regroup_wt=w_in_t, regroup_layer=i + 1)
        else:
            o_d = _mamba2_ssd(*ssd_args, bsz=bsz, seq=seq, rows=ssd_rows)
            if i + 1 < DEPTH:
                w_small = _regroup_mixer_weights(w_in_t, i + 1)
        merged, w_out_b = _merge(xb, (o_a, o_b, o_c, o_d), w_gates, w_branch, w_out, i, tm=512, tn=256)
        s1 = _matmul(merged, w_out_b, tm=512, tn=1024, out_dtype=F32, resid=xf, name="out_proj")
        xf, xb = _layer_norm(s1, ln1_g[i], ln1_b[i], tr=512)
        act, w_down_b = _swiglu(xb, w_gate_up, w_down, i, tm=1024, tn=256)
        s2 = _matmul(act, w_down_b, tm=512, tn=512, out_dtype=F32, resid=xf, name="down_proj")
        xf, xb = _layer_norm(s2, ln2_g[i], ln2_b[i], tr=512)
    return xf.reshape(bsz, seq, d)
```

```python
import functools

import jax
import jax.numpy as jnp
import numpy as np
from jax import lax
from jax.experimental import pallas as pl
from jax.experimental.pallas import tpu as pltpu

F32 = jnp.float32
BF16 = jnp.bfloat16

D_MODEL = 4096
DEPTH = 2
N_BRANCH = 4
BRANCH_WIDTH = D_MODEL // 4

GDN_HEAD_DIM = 128
GDN_HEADS = BRANCH_WIDTH // GDN_HEAD_DIM
GDN_WIDTH = GDN_HEADS * GDN_HEAD_DIM
CHUNK = 64
SHORT_CONV = 4

SWA_HEAD_DIM = 128
SWA_Q_HEADS = BRANCH_WIDTH // SWA_HEAD_DIM
SWA_KV_HEADS = SWA_Q_HEADS // 4
SWA_REP = SWA_Q_HEADS // SWA_KV_HEADS
SWA_WINDOW = 128
ROPE_THETA = 10000.0

CONF_WIDTH = BRANCH_WIDTH
CONF_CONV_WIDTH = 31

SSD_HEAD_DIM = 64
SSD_WIDTH = BRANCH_WIDTH
SSD_HEADS = SSD_WIDTH // SSD_HEAD_DIM
SSD_GROUPS = 2
SSD_HEADS_PER_GROUP = SSD_HEADS // SSD_GROUPS
SSD_STATE = 128
SSD_BC_WIDTH = 2 * SSD_GROUPS * SSD_STATE

FFN_HIDDEN = -(-8 * D_MODEL // (3 * 256)) * 256

ALPHA = (2.0 * DEPTH) ** 0.25
LN_EPS = 1e-5
RMS_EPS = 1e-6

LANES = 128
VMEM_LIMIT = 52 * 1024 * 1024

COL_QKV = 0
COL_GATE = COL_QKV + 3 * GDN_WIDTH
COL_SWA_Q = COL_GATE + GDN_WIDTH
COL_CONF = COL_SWA_Q + BRANCH_WIDTH
COL_Z = COL_CONF + 2 * CONF_WIDTH
COL_XS = COL_Z + SSD_WIDTH
COL_BC = COL_XS + SSD_WIDTH
COL_SWA_K = COL_BC + SSD_BC_WIDTH
COL_SWA_V = COL_SWA_K + SWA_KV_HEADS * SWA_HEAD_DIM
COL_SMALL = COL_SWA_V + SWA_KV_HEADS * SWA_HEAD_DIM
SMALL_A = 0
SMALL_B = GDN_HEADS
SMALL_DT = 2 * GDN_HEADS
PROJ_COLS = 10752

_REF_SPLITS = (3 * GDN_WIDTH, GDN_WIDTH, GDN_HEADS, GDN_HEADS, BRANCH_WIDTH, 256, 256,
               2 * CONF_WIDTH, SSD_WIDTH, SSD_WIDTH + SSD_BC_WIDTH, SSD_HEADS, N_BRANCH * D_MODEL)
_REF_OFFS = np.concatenate([[0], np.cumsum(_REF_SPLITS)]).tolist()


def _params(sem, vmem=VMEM_LIMIT):
    return pltpu.CompilerParams(dimension_semantics=sem, vmem_limit_bytes=vmem)


def _bdot(a, b):
    return jnp.dot(a.astype(BF16), b.astype(BF16), preferred_element_type=F32)


def _bdot_nt(a, b):
    return lax.dot_general(a.astype(BF16), b.astype(BF16), (((1,), (1,)), ((), ())),
                           preferred_element_type=F32)


def _silu(x):
    return x * jax.nn.sigmoid(x)


def _softplus(x):
    return jnp.maximum(x, 0.0) + jnp.log(1.0 + jnp.exp(-jnp.abs(x)))


def _cast_weights_once(w_refs, wb_refs):
    @pl.when(pl.program_id(1) == 0)
    def _():
        for w_ref, wb_ref in zip(w_refs, wb_refs):
            wb_ref[...] = w_ref[...].astype(BF16)


def _mm_kernel(x_ref, w_ref, *rest, has_resid):
    r_ref, o_ref = rest if has_resid else (None, rest[0])
    acc = jnp.dot(x_ref[...], w_ref[...], preferred_element_type=F32)
    if has_resid:
        acc = ALPHA * r_ref[...] + acc
    o_ref[...] = acc.astype(o_ref.dtype)


def _matmul(x, w, *, tm, tn, out_dtype, resid=None, name):
    m, k = x.shape
    n = w.shape[1]
    assert x.dtype == BF16 and w.dtype == BF16
    in_specs = [pl.BlockSpec((tm, k), lambda j, i: (i, 0)),
                pl.BlockSpec((k, tn), lambda j, i: (0, j))]
    args = [x, w]
    if resid is not None:
        in_specs.append(pl.BlockSpec((tm, tn), lambda j, i: (i, j)))
        args.append(resid)
    return pl.pallas_call(
        functools.partial(_mm_kernel, has_resid=resid is not None),
        out_shape=jax.ShapeDtypeStruct((m, n), out_dtype),
        grid=(n // tn, m // tm), in_specs=in_specs,
        out_specs=pl.BlockSpec((tm, tn), lambda j, i: (i, j)),
        compiler_params=_params(("parallel", "parallel")), name=name)(*args)


def _swiglu_kernel(x_ref, wg0_ref, wg1_ref, wu0_ref, wu1_ref, wd_ref, o_ref, wdb_ref, wgb_ref, wub_ref):
    half = wg0_ref.shape[0]
    _cast_weights_once([wg0_ref, wg1_ref, wu0_ref, wu1_ref],
                       [wgb_ref.at[0:half], wgb_ref.at[half:2 * half], wub_ref.at[0:half], wub_ref.at[half:2 * half]])
    x = x_ref[...]
    g = jnp.dot(x, wgb_ref[...], preferred_element_type=F32)
    u = jnp.dot(x, wub_ref[...], preferred_element_type=F32)
    o_ref[...] = (_silu(g) * u).astype(o_ref.dtype)
    wdb_ref[...] = wd_ref[...].astype(BF16)


def _swiglu(x, w_gate_up, w_down, layer, *, tm, tn):
    m, k = x.shape
    hidden = w_gate_up.shape[2] // 2
    nb = hidden // tn
    n_inner = m // tm
    d_out = w_down.shape[2]
    wd_rows = hidden // (nb * n_inner)
    assert wd_rows * nb * n_inner == hidden and wd_rows % 16 == 0

    def weight_half(which, up, order):
        def index(j, i):
            ahead = jnp.minimum(j + jnp.where(i > order, 1, 0), nb - 1)
            return (layer, which, ahead + (nb if up else 0))
        return pl.BlockSpec((None, k // 2, tn), index)

    return pl.pallas_call(
        _swiglu_kernel,
        out_shape=(jax.ShapeDtypeStruct((m, hidden), BF16), jax.ShapeDtypeStruct((hidden, d_out), BF16)),
        grid=(nb, n_inner),
        in_specs=[pl.BlockSpec((tm, k), lambda j, i: (i, 0)),
                  weight_half(0, False, 0), weight_half(1, False, 1),
                  weight_half(0, True, 2), weight_half(1, True, 3),
                  pl.BlockSpec((None, wd_rows, d_out), lambda j, i: (layer, j * n_inner + i, 0))],
        out_specs=(pl.BlockSpec((tm, tn), lambda j, i: (i, j)),
                   pl.BlockSpec((wd_rows, d_out), lambda j, i: (j * n_inner + i, 0))),
        scratch_shapes=[pltpu.VMEM((k, tn), BF16), pltpu.VMEM((k, tn), BF16)],
        compiler_params=_params(("arbitrary", "arbitrary")), name="swiglu",
    )(x, *([w_gate_up] * 4), w_down)


def _merge_kernel(x_ref, oa_ref, ob_ref, oc_ref, od_ref,
                  wg0, wg1, wg2, wg3, wb0, wb1, wb2, wb3, wo_ref, o_ref, wob_ref, wbb0, wbb1, wbb2, wbb3):
    _cast_weights_once([wb0, wb1, wb2, wb3], [wbb0, wbb1, wbb2, wbb3])
    wob_ref[...] = wo_ref[...].astype(BF16)
    x = x_ref[...]
    acc = None
    for o_r, wg, wb in ((oa_ref, wg0, wbb0), (ob_ref, wg1, wbb1), (oc_ref, wg2, wbb2), (od_ref, wg3, wbb3)):
        gate = jax.nn.sigmoid(jnp.dot(x, wg[...], preferred_element_type=F32))
        up = jnp.dot(o_r[...], wb[...], preferred_element_type=F32)
        acc = gate * up if acc is None else acc + gate * up
    o_ref[...] = acc.astype(o_ref.dtype)


def _merge(x, branches, w_gates, w_branch, w_out, layer, *, tm, tn):
    m, k = x.shape
    d = w_branch.shape[3]
    nb = d // tn
    bw = w_branch.shape[2]
    n_inner = m // tm
    wo_rows = w_out.shape[1] // (nb * n_inner)
    assert wo_rows * nb * n_inner == w_out.shape[1] and wo_rows % 16 == 0
    in_specs = [pl.BlockSpec((tm, k), lambda j, i: (i, 0))]
    in_specs += [pl.BlockSpec((tm, bw), lambda j, i: (i, 0)) for _ in range(N_BRANCH)]
    in_specs += [pl.BlockSpec((k, tn), functools.partial(lambda j, i, g: (0, g * nb + j), g=g))
                 for g in range(N_BRANCH)]
    in_specs += [pl.BlockSpec((None, None, bw, tn), functools.partial(
        lambda j, i, g: (layer, g, 0, jnp.minimum(j + jnp.where(i > g, 1, 0), nb - 1)), g=g))
                 for g in range(N_BRANCH)]
    in_specs.append(pl.BlockSpec((None, wo_rows, w_out.shape[2]), lambda j, i: (layer, j * n_inner + i, 0)))
    return pl.pallas_call(
        _merge_kernel,
        out_shape=(jax.ShapeDtypeStruct((m, d), BF16), jax.ShapeDtypeStruct(w_out.shape[1:], BF16)),
        grid=(nb, n_inner), in_specs=in_specs,
        out_specs=(pl.BlockSpec((tm, tn), lambda j, i: (i, j)),
                   pl.BlockSpec((wo_rows, w_out.shape[2]), lambda j, i: (j * n_inner + i, 0))),
        scratch_shapes=[pltpu.VMEM((bw, tn), BF16) for _ in range(N_BRANCH)],
        compiler_params=_params(("parallel", "arbitrary")), name="merge",
    )(x, *branches, *([w_gates] * N_BRANCH), *([w_branch] * N_BRANCH), w_out)


def _ln_kernel(s_ref, g_ref, b_ref, of_ref, ob_ref):
    s = s_ref[...]
    mu = jnp.mean(s, axis=-1, keepdims=True)
    c = s - mu
    var = jnp.mean(c * c, axis=-1, keepdims=True)
    y = c * lax.rsqrt(var + LN_EPS) * g_ref[...] + b_ref[...]
    of_ref[...] = y
    ob_ref[...] = y.astype(BF16)


def _layer_norm(s, g, b, *, tr):
    m, d = s.shape
    row = pl.BlockSpec((tr, d), lambda i: (i, 0))
    vec = pl.BlockSpec((1, d), lambda i: (0, 0))
    return pl.pallas_call(
        _ln_kernel,
        out_shape=(jax.ShapeDtypeStruct((m, d), F32), jax.ShapeDtypeStruct((m, d), BF16)),
        grid=(m // tr,), in_specs=[row, vec, vec], out_specs=(row, row),
        compiler_params=_params(("parallel",)), name="layer_norm")(s, g.reshape(1, d), b.reshape(1, d))


def _causal_conv(pad_ref, x, w_ref, halo, t_first, shift_ref=None):
    rows = x.shape[0]
    k = w_ref.shape[0]

    @pl.when(t_first)
    def _():
        pad_ref[0:halo, :] = jnp.zeros((halo, x.shape[1]), F32)

    pad_ref[halo:halo + rows, :] = x
    acc = None
    sublanes = 8
    first = halo - (k - 1)
    for phase in range(min(sublanes, k)):
        taps = list(range(phase, k, sublanes))
        lo = first + taps[0]
        if shift_ref is not None:
            span = rows + taps[-1] - taps[0]
            shift_ref[0:span, :] = pad_ref[lo:lo + span, :]
        for j in taps:
            if shift_ref is not None:
                window = shift_ref[j - taps[0]:j - taps[0] + rows, :]
            else:
                window = pad_ref[first + j:first + j + rows, :]
            term = w_ref[j:j + 1, :] * window
            acc = term if acc is None else acc + term
    pad_ref[0:halo, :] = x[rows - halo:rows, :]
    return acc


def _chunk_cumsum(x):
    r = lax.broadcasted_iota(jnp.int32, (CHUNK, CHUNK), 0)
    c = lax.broadcasted_iota(jnp.int32, (CHUNK, CHUNK), 1)
    tril = jnp.where(r >= c, 1.0, 0.0).astype(F32)
    return jnp.dot(tril, x, preferred_element_type=F32, precision=lax.Precision.HIGHEST)


def _bmm(a, b):
    return lax.dot_general(a.astype(BF16), b.astype(BF16), (((2,), (1,)), ((0,), (0,))),
                           preferred_element_type=F32)


def _bmm_nt(a, b):
    return lax.dot_general(a.astype(BF16), b.astype(BF16), (((2,), (2,)), ((0,), (0,))),
                           preferred_element_type=F32)


def _bmm_tn(a, b):
    return lax.dot_general(a.astype(BF16), b.astype(BF16), (((1,), (1,)), ((0,), (0,))),
                           preferred_element_type=F32)


def _unit_lower_inverse(a):
    r = lax.broadcasted_iota(jnp.int32, (1, CHUNK, CHUNK), 1)
    c = lax.broadcasted_iota(jnp.int32, (1, CHUNK, CHUNK), 2)
    same16 = (r >> 4) == (c >> 4)
    same32 = (r >> 5) == (c >> 5)
    eye = jnp.where(r == c, 1.0, 0.0).astype(F32)
    ad = jnp.where(same16, a, 0.0)
    o1 = jnp.where(same32, a, 0.0) - ad
    o2 = jnp.where(same32, 0.0, a)
    a2 = _bmm(ad, ad)
    d = eye - ad
    a4 = _bmm(a2, a2)
    d = d + _bmm(d, a2)
    a8 = _bmm(a4, a4)
    d = d + _bmm(d, a4)
    d = d + _bmm(d, a8)
    d = d - _bmm(d, _bmm(o1, d))
    d = d - _bmm(d, _bmm(o2, d))
    return d


def _causal_masks(lead=()):
    shape = lead + (CHUNK, CHUNK)
    r = lax.broadcasted_iota(jnp.int32, shape, len(lead))
    c = lax.broadcasted_iota(jnp.int32, shape, len(lead) + 1)
    return r >= c, r > c


def _decay_matrix(col, row, causal):
    return jnp.where(causal, jnp.exp(jnp.where(causal, col - row, 0.0)), 0.0)


def _gdn_kernel(qkv_ref, gate_ref, small_ref, convw_ref, nexpa_ref, dtb_ref, normw_ref,
                o_ref, pad_ref, state_ref, *, rows):
    t_first = pl.program_id(1) == 0

    @pl.when(t_first)
    def _():
        state_ref[...] = jnp.zeros(state_ref.shape, F32)

    y = _silu(_causal_conv(pad_ref, qkv_ref[...], convw_ref, 8, t_first))
    small = small_ref[...]
    g_all = nexpa_ref[...] * _softplus(small + dtb_ref[...])
    beta_all = jax.nn.sigmoid(small)
    causal, strict = _causal_masks((1,))
    nch = rows // CHUNK
    gcums = [_chunk_cumsum(g_all[ci * CHUNK:(ci + 1) * CHUNK, :]) for ci in range(nch)]
    gcums_t = [g.T for g in gcums]

    def stack(fn):
        return jnp.stack([fn(ci, h) for ci in range(nch) for h in range(GDN_HEADS)], axis=0)

    def head_cols(base):
        return stack(lambda ci, h: y[ci * CHUNK:(ci + 1) * CHUNK,
                                     base + h * GDN_HEAD_DIM:base + (h + 1) * GDN_HEAD_DIM])

    q, k, v = head_cols(0), head_cols(GDN_WIDTH), head_cols(2 * GDN_WIDTH)
    q = q * lax.rsqrt(jnp.sum(q * q, axis=-1, keepdims=True) + RMS_EPS) * (GDN_HEAD_DIM ** -0.5)
    k = k * lax.rsqrt(jnp.sum(k * k, axis=-1, keepdims=True) + RMS_EPS)
    beta = stack(lambda ci, h: beta_all[ci * CHUNK:(ci + 1) * CHUNK, SMALL_B + h:SMALL_B + h + 1])
    gc = stack(lambda ci, h: gcums[ci][:, SMALL_A + h:SMALL_A + h + 1])
    gr = stack(lambda ci, h: gcums_t[ci][SMALL_A + h:SMALL_A + h + 1, :])
    g_last = gc[:, CHUNK - 1:CHUNK, :]
    decay = _decay_matrix(gc, gr, causal)
    kb = k * beta
    t_mat = _unit_lower_inverse(jnp.where(strict, _bmm_nt(kb, k) * decay, 0.0))
    e_gc = jnp.exp(gc)
    u = _bmm(t_mat, v * beta)
    w = _bmm(t_mat, kb * e_gc)
    qk = _bmm_nt(q, k) * decay
    q_dec = q * e_gc
    k_dec = k * jnp.exp(g_last - gc)
    chunk_decay = jnp.exp(g_last)

    norm_w = normw_ref[...]
    state = state_ref[...]
    for ci in range(nch):
        sl = slice(ci * GDN_HEADS, (ci + 1) * GDN_HEADS)
        v_new = u[sl] - _bmm(w[sl], state)
        o = _bmm(q_dec[sl], state) + _bmm(qk[sl], v_new)
        state = state * chunk_decay[sl] + _bmm_tn(k_dec[sl], v_new)
        o = o * lax.rsqrt(jnp.mean(o * o, axis=-1, keepdims=True) + RMS_EPS) * norm_w
        for h in range(GDN_HEADS):
            c0 = h * GDN_HEAD_DIM
            gate = gate_ref[ci * CHUNK:(ci + 1) * CHUNK, c0:c0 + GDN_HEAD_DIM]
            o_ref[ci * CHUNK:(ci + 1) * CHUNK, c0:c0 + GDN_HEAD_DIM] = (o[h] * _silu(gate)).astype(o_ref.dtype)
    state_ref[...] = state


def _gated_deltanet(proj, conv_w, a_log, dt_bias, norm_w, *, bsz, seq, rows=256):
    nt = seq // rows
    nexpa = jnp.zeros((1, LANES), F32).at[0, SMALL_A:SMALL_A + GDN_HEADS].set(-jnp.exp(a_log))
    dtb = jnp.zeros((1, LANES), F32).at[0, SMALL_A:SMALL_A + GDN_HEADS].set(dt_bias)

    def rowblk(width, col):
        return pl.BlockSpec((rows, width), lambda b, t: (b * nt + t, col // width))

    def const(shape):
        return pl.BlockSpec(shape, lambda b, t: (0,) * len(shape))

    return pl.pallas_call(
        functools.partial(_gdn_kernel, rows=rows),
        out_shape=jax.ShapeDtypeStruct((bsz * seq, GDN_WIDTH), BF16),
        grid=(bsz, nt),
        in_specs=[rowblk(3 * GDN_WIDTH, COL_QKV), rowblk(GDN_WIDTH, COL_GATE), rowblk(LANES, COL_SMALL),
                  const((SHORT_CONV, 3 * GDN_WIDTH)), const((1, LANES)), const((1, LANES)),
                  const((1, GDN_HEAD_DIM))],
        out_specs=pl.BlockSpec((rows, GDN_WIDTH), lambda b, t: (b * nt + t, 0)),
        scratch_shapes=[pltpu.VMEM((8 + rows, 3 * GDN_WIDTH), F32),
                        pltpu.VMEM((GDN_HEADS, GDN_HEAD_DIM, GDN_HEAD_DIM), F32)],
        compiler_params=_params(("parallel", "arbitrary")), name="gated_deltanet",
    )(proj, proj, proj, conv_w, nexpa, dtb, norm_w.reshape(1, GDN_HEAD_DIM))


def _swa_kernel(sinks_ref, q_ref, kc_ref, kp_ref, vc_ref, vp_ref, cos_ref, sin_ref, cosp_ref, sinp_ref,
                o_ref):
    n = pl.program_id(1)
    blk = SWA_WINDOW
    cos, sin = cos_ref[...], sin_ref[...]
    cosp, sinp = cosp_ref[...], sinp_ref[...]

    def rope(x, c, s):
        return x * c + pltpu.roll(x, SWA_HEAD_DIM // 2, axis=1) * s

    rows = SWA_REP * blk
    qi = lax.broadcasted_iota(jnp.int32, (rows, 2 * blk), 0) & (blk - 1)
    kj = lax.broadcasted_iota(jnp.int32, (rows, 2 * blk), 1)
    dist = qi + blk - kj
    in_window = jnp.where(dist >= 0, jnp.where(dist < SWA_WINDOW, 1, 0), 0)
    has_prev = jnp.where(kj >= blk, 1, jnp.where(n > 0, 1, 0))
    mask = (in_window * has_prev) > 0
    head_of_row = lax.broadcasted_iota(jnp.int32, (rows, 1), 0) >> 7

    for g in range(SWA_KV_HEADS):
        d0 = g * SWA_HEAD_DIM
        q_rows = [rope(q_ref[:, (g * SWA_REP + j) * SWA_HEAD_DIM:(g * SWA_REP + j + 1) * SWA_HEAD_DIM], cos, sin)
                  for j in range(SWA_REP)]
        qs = jnp.concatenate(q_rows, axis=0)
        kband = jnp.concatenate([rope(kp_ref[:, d0:d0 + SWA_HEAD_DIM], cosp, sinp),
                                 rope(kc_ref[:, d0:d0 + SWA_HEAD_DIM], cos, sin)], axis=0)
        vband = jnp.concatenate([vp_ref[:, d0:d0 + SWA_HEAD_DIM], vc_ref[:, d0:d0 + SWA_HEAD_DIM]], axis=0)
        s = _bdot_nt(qs, kband) * (SWA_HEAD_DIM ** -0.5)
        s = jnp.where(mask, s, -jnp.inf)
        sink = jnp.zeros((rows, 1), F32)
        for j in range(SWA_REP):
            sink = jnp.where(head_of_row == j, sinks_ref[g * SWA_REP + j], sink)
        m = jnp.maximum(jnp.max(s, axis=-1, keepdims=True), sink)
        p = jnp.exp(s - m)
        denom = jnp.sum(p, axis=-1, keepdims=True) + jnp.exp(sink - m)
        o = _bdot(p / denom, vband)
        for j in range(SWA_REP):
            c0 = (g * SWA_REP + j) * SWA_HEAD_DIM
            o_ref[:, c0:c0 + SWA_HEAD_DIM] = o[j * blk:(j + 1) * blk, :].astype(o_ref.dtype)


def _sliding_window_attention(proj, sinks, cos_full, sin_signed, *, bsz, seq):
    blk = SWA_WINDOW
    nb = seq // blk
    kvw = SWA_KV_HEADS * SWA_HEAD_DIM

    def cur(width, col):
        return pl.BlockSpec((blk, width), lambda b, n: (b * nb + n, col // width))

    def prev(width, col):
        return pl.BlockSpec((blk, width), lambda b, n: (b * nb + jnp.maximum(n - 1, 0), col // width))

    tab_cur = pl.BlockSpec((blk, SWA_HEAD_DIM), lambda b, n: (n, 0))
    tab_prev = pl.BlockSpec((blk, SWA_HEAD_DIM), lambda b, n: (jnp.maximum(n - 1, 0), 0))
    return pl.pallas_call(
        _swa_kernel, out_shape=jax.ShapeDtypeStruct((bsz * seq, BRANCH_WIDTH), BF16),
        grid=(bsz, nb),
        in_specs=[pl.BlockSpec(memory_space=pltpu.SMEM),
                  cur(BRANCH_WIDTH, COL_SWA_Q), cur(kvw, COL_SWA_K), prev(kvw, COL_SWA_K),
                  cur(kvw, COL_SWA_V), prev(kvw, COL_SWA_V), tab_cur, tab_cur, tab_prev, tab_prev],
        out_specs=pl.BlockSpec((blk, BRANCH_WIDTH), lambda b, n: (b * nb + n, 0)),
        compiler_params=_params(("parallel", "parallel")), name="sliding_window_attention",
    )(sinks, proj, proj, proj, proj, proj, cos_full, sin_signed, cos_full, sin_signed)


def _conf_kernel(a_ref, gt_ref, convw_ref, convb_ref, lng_ref, lnb_ref, pww_ref, pwb_ref, wt_ref,
                 o_ref, wg_ref, pad_ref, shift_ref):
    wg_ref[...] = wt_ref[0].T.astype(BF16)
    t_first = pl.program_id(1) == 0
    h = a_ref[...] * jax.nn.sigmoid(gt_ref[...])
    h = _causal_conv(pad_ref, h, convw_ref, 32, t_first, shift_ref) + convb_ref[...]
    mu = jnp.mean(h, axis=-1, keepdims=True)
    c = h - mu
    var = jnp.mean(c * c, axis=-1, keepdims=True)
    h = _silu(c * lax.rsqrt(var + LN_EPS) * lng_ref[...] + lnb_ref[...])
    o_ref[...] = (jnp.dot(h.astype(BF16), pww_ref[...], preferred_element_type=F32)
                  + pwb_ref[...]).astype(o_ref.dtype)


def _conformer_conv(proj, conv_w, conv_b, ln_g, ln_b, pw_w, pw_b, wt, layer, *, bsz, seq, rows=256):
    nt = seq // rows
    w = CONF_WIDTH
    d = wt.shape[2]
    gate_cols = N_BRANCH * D_MODEL
    gate_rows = gate_cols // (bsz * nt)
    assert gate_rows * bsz * nt == gate_cols and gate_rows % LANES == 0
    gate0 = _REF_OFFS[11]

    def rowblk(col):
        return pl.BlockSpec((rows, w), lambda b, t: (b * nt + t, col // w))

    def const(shape):
        return pl.BlockSpec(shape, lambda b, t: (0,) * len(shape))

    wt_spec = pl.BlockSpec((pl.Element(1), pl.Element(gate_rows), pl.Element(d)),
                           lambda b, t: (layer, pl.multiple_of(gate0 + (b * nt + t) * gate_rows, 8), 0))
    return pl.pallas_call(
        _conf_kernel,
        out_shape=(jax.ShapeDtypeStruct((bsz * seq, w), BF16), jax.ShapeDtypeStruct((d, gate_cols), BF16)),
        grid=(bsz, nt),
        in_specs=[rowblk(COL_CONF), rowblk(COL_CONF + w), const((CONF_CONV_WIDTH, w)), const((1, w)),
                  const((1, w)), const((1, w)), const((w, w)), const((1, w)), wt_spec],
        out_specs=(pl.BlockSpec((rows, w), lambda b, t: (b * nt + t, 0)),
                   pl.BlockSpec((d, gate_rows), lambda b, t: (0, b * nt + t))),
        scratch_shapes=[pltpu.VMEM((32 + rows, w), F32),
                        pltpu.VMEM((rows + 8 * ((CONF_CONV_WIDTH - 1) // 8), w), F32)],
        compiler_params=_params(("parallel", "arbitrary")), name="conformer_conv",
    )(proj, proj, conv_w, conv_b.reshape(1, w), ln_g.reshape(1, w), ln_b.reshape(1, w),
      pw_w.astype(BF16), pw_b.reshape(1, w), wt)


def _ssd_kernel(z_ref, xs_ref, bc_ref, small_ref, wx_ref, wbc_ref, bx_ref, bbc_ref, nexpa_ref, dtb_ref,
                dskip_ref, normw_ref, *rest, rows, regroup):
    if regroup:
        wt_ref, wdt_ref, o_ref, wsmall_ref, xpad_ref, bcpad_ref, state_ref, y_ref = rest
        step = pl.program_id(0) * pl.num_programs(1) + pl.program_id(1)
        _regroup_mixer_block(step, wt_ref, wdt_ref, wsmall_ref)
    else:
        o_ref, xpad_ref, bcpad_ref, state_ref, y_ref = rest
    t_first = pl.program_id(1) == 0

    @pl.when(t_first)
    def _():
        state_ref[...] = jnp.zeros(state_ref.shape, F32)

    xs = _silu(_causal_conv(xpad_ref, xs_ref[...], wx_ref, 8, t_first) + bx_ref[...])
    bc = _silu(_causal_conv(bcpad_ref, bc_ref[...], wbc_ref, 8, t_first) + bbc_ref[...])
    dt_all = _softplus(small_ref[...] + dtb_ref[...])
    a_all = nexpa_ref[...] * dt_all
    gs = SSD_GROUPS * SSD_STATE
    nch = rows // CHUNK
    pairs = SSD_HEADS // 2
    pairs_per_group = pairs // SSD_GROUPS

    er = lax.broadcasted_iota(jnp.int32, (LANES, SSD_WIDTH), 0)
    ec = lax.broadcasted_iota(jnp.int32, (LANES, SSD_WIDTH), 1)
    expand = jnp.where(er == SMALL_DT + (ec >> 6), 1.0, 0.0).astype(F32)

    def per_head_cols(v):
        return jnp.dot(v, expand, preferred_element_type=F32, precision=lax.Precision.HIGHEST)

    xdt = xs * per_head_cols(dt_all)
    lane = lax.broadcasted_iota(jnp.int32, (1, LANES), 1)
    low = lane < SSD_HEAD_DIM
    row2 = lax.broadcasted_iota(jnp.int32, (2 * CHUNK, 1), 0)
    l_idx = lax.broadcasted_iota(jnp.int32, (CHUNK, LANES), 0)
    s_idx = lax.broadcasted_iota(jnp.int32, (CHUNK, LANES), 1) & (CHUNK - 1)
    causal2 = l_idx >= s_idx

    lhs, rhs, xdec, bms, cms, e_ac, h_dec = [], [], [], [], [], [], []
    for ci in range(nch):
        r0 = ci * CHUNK
        acs = _chunk_cumsum(a_all[r0:r0 + CHUNK, :])
        ac_cols = per_head_cols(acs)
        acs_t = jnp.concatenate([acs, acs], axis=0).T
        a_last_cols = ac_cols[CHUNK - 1:CHUNK, :]
        xdec_c = xdt[r0:r0 + CHUNK, :] * jnp.exp(a_last_cols - ac_cols)
        e_ac_c = jnp.exp(ac_cols)
        cb2 = []
        for g in range(SSD_GROUPS):
            bm = bc[r0:r0 + CHUNK, g * SSD_STATE:(g + 1) * SSD_STATE]
            cm = bc[r0:r0 + CHUNK, gs + g * SSD_STATE:gs + (g + 1) * SSD_STATE]
            cb2.append(_bdot_nt(cm, jnp.concatenate([bm, bm], axis=0)))
        for p in range(pairs):
            g = p // pairs_per_group
            c0 = p * LANES
            l0 = SMALL_DT + 2 * p
            a_row = jnp.where(low, acs_t[l0:l0 + 1, :], acs_t[l0 + 1:l0 + 2, :])
            seg = _decay_matrix(ac_cols[:, c0:c0 + LANES], a_row, causal2)
            lhs.append(cb2[g] * seg)
            xp = xdt[r0:r0 + CHUNK, c0:c0 + LANES]
            rhs.append(jnp.concatenate([jnp.where(low, xp, 0.0), jnp.where(low, 0.0, xp)], axis=0))
            xdec.append(xdec_c[:, c0:c0 + LANES])
            bms.append(bc[r0:r0 + CHUNK, g * SSD_STATE:(g + 1) * SSD_STATE])
            cms.append(bc[r0:r0 + CHUNK, gs + g * SSD_STATE:gs + (g + 1) * SSD_STATE])
            e_ac.append(e_ac_c[:, c0:c0 + LANES])
            a_last = jnp.where(row2 < SSD_HEAD_DIM, acs[CHUNK - 1:CHUNK, l0:l0 + 1],
                               acs[CHUNK - 1:CHUNK, l0 + 1:l0 + 2])
            h_dec.append(jnp.exp(a_last))

    y_diag = _bmm(jnp.stack(lhs), jnp.stack(rhs))
    states = _bmm_tn(jnp.stack(xdec), jnp.stack(bms))
    cms, e_ac, h_dec = jnp.stack(cms), jnp.stack(e_ac), jnp.stack(h_dec)

    h = state_ref[...]
    for ci in range(nch):
        sl = slice(ci * pairs, (ci + 1) * pairs)
        y = y_diag[sl] + _bmm_nt(cms[sl], h) * e_ac[sl]
        h = h * h_dec[sl] + states[sl]
        for p in range(pairs):
            y_ref[ci * CHUNK:(ci + 1) * CHUNK, p * LANES:(p + 1) * LANES] = y[p]
    state_ref[...] = h

    y = y_ref[...] + dskip_ref[...] * xs
    y = y * _silu(z_ref[...])
    y = y * lax.rsqrt(jnp.mean(y * y, axis=-1, keepdims=True) + RMS_EPS) * normw_ref[...]
    o_ref[...] = y.astype(o_ref.dtype)


def _mamba2_ssd(proj, conv_w, conv_b, a_log, dt_bias, d_skip, norm_w, *, bsz, seq, rows=256,
                regroup_wt=None, regroup_layer=None):
    nt = seq // rows
    w = SSD_WIDTH
    regroup = regroup_wt is not None
    assert not regroup or bsz * nt >= MIXER_BLOCKS
    nexpa = jnp.zeros((1, LANES), F32).at[0, SMALL_DT:SMALL_DT + SSD_HEADS].set(-jnp.exp(a_log))
    dtb = jnp.zeros((1, LANES), F32).at[0, SMALL_DT:SMALL_DT + SSD_HEADS].set(dt_bias)
    d_full = jnp.repeat(d_skip, SSD_HEAD_DIM).reshape(1, w)

    def rowblk(width, col):
        return pl.BlockSpec((rows, width), lambda b, t: (b * nt + t, col // width))

    def const(shape):
        return pl.BlockSpec(shape, lambda b, t: (0,) * len(shape))

    in_specs = [rowblk(w, COL_Z), rowblk(w, COL_XS), rowblk(SSD_BC_WIDTH, COL_BC), rowblk(LANES, COL_SMALL),
                const((SHORT_CONV, w)), const((SHORT_CONV, SSD_BC_WIDTH)), const((1, w)),
                const((1, SSD_BC_WIDTH)), const((1, LANES)), const((1, LANES)), const((1, w)), const((1, w))]
    args = [proj, proj, proj, proj, conv_w[:, :w], conv_w[:, w:], conv_b[:w].reshape(1, w),
            conv_b[w:].reshape(1, SSD_BC_WIDTH), nexpa, dtb, d_full, norm_w.reshape(1, w)]
    out_shape = jax.ShapeDtypeStruct((bsz * seq, w), BF16)
    out_specs = pl.BlockSpec((rows, w), lambda b, t: (b * nt + t, 0))
    if regroup:
        d = regroup_wt.shape[2]
        wt_specs, wsmall_spec = _mixer_weight_specs(d, regroup_layer, lambda b, t: b * nt + t)
        in_specs += wt_specs
        args += [regroup_wt, regroup_wt]
        out_shape = (out_shape, jax.ShapeDtypeStruct((d, PROJ_COLS), BF16))
        out_specs = (out_specs, wsmall_spec)
    return pl.pallas_call(
        functools.partial(_ssd_kernel, rows=rows, regroup=regroup),
        out_shape=out_shape, grid=(bsz, nt), in_specs=in_specs, out_specs=out_specs,
        scratch_shapes=[pltpu.VMEM((8 + rows, w), F32), pltpu.VMEM((8 + rows, SSD_BC_WIDTH), F32),
                        pltpu.VMEM((SSD_HEADS // 2, 2 * SSD_HEAD_DIM, SSD_STATE), F32),
                        pltpu.VMEM((rows, w), F32)],
        compiler_params=_params(("arbitrary", "arbitrary") if regroup else ("parallel", "arbitrary")),
        name="mamba2_ssd")(*args)


PREP_TN = 512
MIXER_BLOCKS = PROJ_COLS // PREP_TN
SMALL_BLOCK = COL_SMALL // PREP_TN


def _mixer_weight_rows(j):
    o = _REF_OFFS
    q0, conf0, kv = (c // PREP_TN for c in (COL_SWA_Q, COL_CONF, COL_SWA_K))
    c = j * PREP_TN
    return jnp.where(j < q0, c, jnp.where(j < conf0, c + (o[4] - COL_SWA_Q),
                     jnp.where(j < kv, c + (o[7] - COL_CONF), jnp.where(j < SMALL_BLOCK, o[5], o[2]))))


def _regroup_mixer_block(j, wt_ref, dt_ref, o_ref):
    @pl.when(j < SMALL_BLOCK)
    def _():
        o_ref[...] = wt_ref[0].T.astype(BF16)

    @pl.when(j == SMALL_BLOCK)
    def _():
        pad = jnp.zeros((PREP_TN - SMALL_DT - SSD_HEADS, wt_ref.shape[2]), F32)
        rows = jnp.concatenate([wt_ref[0, 0:SMALL_DT, :], dt_ref[0], pad], axis=0)
        o_ref[...] = rows.T.astype(BF16)


def _mixer_weight_specs(d, layer, block_of):
    o = _REF_OFFS
    assert o[2] == COL_SWA_Q and o[3] - o[2] == GDN_HEADS == SMALL_B and o[4] - o[2] == SMALL_DT

    def blk(*idx):
        return jnp.minimum(block_of(*idx), MIXER_BLOCKS - 1)

    in_specs = [pl.BlockSpec((pl.Element(1), pl.Element(PREP_TN), pl.Element(d)),
                             lambda *idx: (layer, pl.multiple_of(_mixer_weight_rows(blk(*idx)), 8), 0)),
                pl.BlockSpec((pl.Element(1), pl.Element(SSD_HEADS), pl.Element(d)),
                             lambda *idx: (layer, o[10], 0))]
    out_spec = pl.BlockSpec((d, PREP_TN), lambda *idx: (0, blk(*idx)))
    return in_specs, out_spec


def _regroup_mixer_kernel(wt_ref, dt_ref, o_ref):
    _regroup_mixer_block(pl.program_id(0), wt_ref, dt_ref, o_ref)


def _regroup_mixer_weights(wt, layer):
    d = wt.shape[2]
    in_specs, out_spec = _mixer_weight_specs(d, layer, lambda j: j)
    return pl.pallas_call(
        _regroup_mixer_kernel, out_shape=jax.ShapeDtypeStruct((d, PROJ_COLS), BF16),
        grid=(MIXER_BLOCKS,), in_specs=in_specs, out_specs=out_spec,
        compiler_params=_params(("parallel",)), name="regroup_mixer_weights")(wt, wt)


def _rotary_tables(seq):
    inv_freq = ROPE_THETA ** (-jnp.arange(0, SWA_HEAD_DIM, 2, dtype=F32) / SWA_HEAD_DIM)
    ang = jnp.arange(seq, dtype=F32)[:, None] * inv_freq[None, :]
    cos, sin = jnp.cos(ang), jnp.sin(ang)
    return jnp.concatenate([cos, cos], axis=1), jnp.concatenate([-sin, sin], axis=1)


def kernel(x, w_in, gdn_conv_w, gdn_a_log, gdn_dt_bias, gdn_norm_w, swa_sinks, conf_conv_w, conf_conv_b, conf_ln_g, conf_ln_b, conf_pw_w, conf_pw_b, ssd_conv_w, ssd_conv_b, ssd_a_log, ssd_dt_bias, ssd_d, ssd_norm_w, w_branch, w_out, ln1_g, ln1_b, w_gate_up, w_down, ln2_g, ln2_b):
    bsz, seq, d = x.shape
    m = bsz * seq
    cos_full, sin_signed = _rotary_tables(seq)
    xf = x.reshape(m, d)
    xb = xf.astype(BF16)
    w_in_t = jnp.swapaxes(w_in, 1, 2)
    ssd_rows = 256
    w_small = _regroup_mixer_weights(w_in_t, 0)
    for i in range(DEPTH):
        proj = _matmul(xb, w_small, tm=512, tn=1536, out_dtype=F32, name="in_proj")
        o_a = _gated_deltanet(proj, gdn_conv_w[i], gdn_a_log[i], gdn_dt_bias[i], gdn_norm_w[i],
                              bsz=bsz, seq=seq)
        o_b = _sliding_window_attention(proj, swa_sinks[i], cos_full, sin_signed, bsz=bsz, seq=seq)
        o_c, w_gates = _conformer_conv(proj, conf_conv_w[i], conf_conv_b[i], conf_ln_g[i], conf_ln_b[i],
                                       conf_pw_w[i], conf_pw_b[i], w_in_t, i, bsz=bsz, seq=seq)
        ssd_args = (proj, ssd_conv_w[i], ssd_conv_b[i], ssd_a_log[i], ssd_dt_bias[i], ssd_d[i], ssd_norm_w[i])
        if i + 1 < DEPTH and bsz * (seq // ssd_rows) >= MIXER_BLOCKS:
            o_d, w_small = _mamba2_ssd(*ssd_args, bsz=bsz, seq=seq, rows=ssd_rows,
                                       regroup_wt=w_in_t, regroup_layer=i + 1)
        else:
            o_d = _mamba2_ssd(*ssd_args, bsz=bsz, seq=seq, rows=ssd_rows)
            if i + 1 < DEPTH:
                w_small = _regroup_mixer_weights(w_in_t, i + 1)
        merged, w_out_b = _merge(xb, (o_a, o_b, o_c, o_d), w_gates, w_branch, w_out, i, tm=512, tn=256)
        s1 = _matmul(merged, w_out_b, tm=512, tn=1024, out_dtype=F32, resid=xf, name="out_proj")
        xf, xb = _layer_norm(s1, ln1_g[i], ln1_b[i], tr=512)
        act, w_down_b = _swiglu(xb, w_gate_up, w_down, i, tm=1024, tn=256)
        s2 = _matmul(act, w_down_b, tm=512, tn=512, out_dtype=F32, resid=xf, name="down_proj")
        xf, xb = _layer_norm(s2, ln2_g[i], ln2_b[i], tr=512)
    return xf.reshape(bsz, seq, d)
```
